```python
import math
import jax, jax.numpy as jnp
from jax import lax
import numpy as np

D_MODEL = 2048
BATCH = 2
SEQ = 16384
DEPTH = 4

GRID_W = 64
CTX_LEN = 256
N_MIXERS = 2
N_ATTN_LAYERS = (DEPTH + 1) // 2
N_POOL_LAYERS = DEPTH // 2

DIFF_HEADS = 8
DIFF_HEAD_DIM = 128
DIFF_V_DIM = 2 * DIFF_HEAD_DIM
ATTN_WIDTH = DIFF_HEADS * DIFF_V_DIM
ROPE_THETA = 10000.0
ROPE_AXIS_PAIRS = DIFF_HEAD_DIM // 4
Q_BLOCK = 128

POOL_WINDOWS = (2, 4, 8, 16)
POOL_GROUPS = 4
POOL_WIDTH = D_MODEL
POOL_GROUP_DIM = POOL_WIDTH // POOL_GROUPS

DEEPNORM_ALPHA = (2 * DEPTH) ** 0.25
DEEPNORM_BETA = (8 * DEPTH) ** -0.25
LN_EPS = 1e-5
SUBLN_EPS = 1e-5

kernel_name = 'hybrid_diffattn_pool_deepnorm_dit'


def _layer_norm(x, g, b):
    xf = x.astype(jnp.float32)
    mu = jnp.mean(xf, axis=-1, keepdims=True)
    var = jnp.mean(jnp.square(xf - mu), axis=-1, keepdims=True)
    y = (xf - mu) * lax.rsqrt(var + LN_EPS)
    return (y * g.astype(jnp.float32) + b.astype(jnp.float32)).astype(x.dtype)


def _rotate(t, cos, sin):
    t1, t2 = jnp.split(t, 2, axis=-1)
    return jnp.concatenate([t1 * cos - t2 * sin, t2 * cos + t1 * sin], axis=-1)


def _rope_2d(t, cos_r, sin_r, cos_c, sin_c):
    tr, tc = jnp.split(t, 2, axis=-1)
    return jnp.concatenate([_rotate(tr, cos_r, sin_r), _rotate(tc, cos_c, sin_c)], axis=-1)


def _heads_qk(t):
    b, s, _ = t.shape
    return t.reshape(b, s, DIFF_HEADS, 2, DIFF_HEAD_DIM).transpose(0, 2, 3, 1, 4)


def _heads_v(t):
    b, s, _ = t.shape
    return t.reshape(b, s, DIFF_HEADS, DIFF_V_DIM).transpose(0, 2, 1, 3)


def _diff_attend(q, k, v, lam):
    b, h, _, sq, d = q.shape
    nb = sq // Q_BLOCK
    q = q * (1.0 / math.sqrt(d))
    qb = jnp.moveaxis(q.reshape(b, h, 2, nb, Q_BLOCK, d), 3, 0)

    def block(qi):
        s = jnp.einsum('bhcqd,bhckd->bhcqk', qi, k).astype(jnp.float32)
        p = jax.nn.softmax(s, axis=-1)
        a = (p[:, :, 0] - lam * p[:, :, 1]).astype(v.dtype)
        return jnp.einsum('bhqk,bhkv->bhqv', a, v)

    o = lax.map(block, qb)
    return jnp.moveaxis(o, 0, 2).reshape(b, h, sq, v.shape[-1])


def _diff_out(o, g, w_out, subln_g, lam_init):
    b, h, s, dv = o.shape
    of = o.astype(jnp.float32)
    of = of * lax.rsqrt(jnp.mean(jnp.square(of), axis=-1, keepdims=True) + SUBLN_EPS)
    of = of * subln_g.astype(jnp.float32) * (1.0 - lam_init)
    y = of.transpose(0, 2, 1, 3).reshape(b, s, h * dv).astype(g.dtype)
    return (y * jax.nn.silu(g)) @ w_out


def _attn_mixer(hx, hc, w_in, w_out, lq1, lk1, lq2, lk2, subln_g, lam_init, rope, ctx_out):
    dm = hx.shape[-1]
    lam = (jnp.exp(jnp.sum(lq1.astype(jnp.float32) * lk1.astype(jnp.float32)))
           - jnp.exp(jnp.sum(lq2.astype(jnp.float32) * lk2.astype(jnp.float32))) + lam_init)
    qx, kx, vx, gx = jnp.split(hx @ w_in, 4, axis=-1)
    qx = _rope_2d(_heads_qk(qx), *rope)
    kx = _rope_2d(_heads_qk(kx), *rope)
    vx = _heads_v(vx)
    if ctx_out:
        qc, kc, vc, gc = jnp.split(hc @ w_in, 4, axis=-1)
    else:
        kc, vc = jnp.split(hc @ w_in[:, dm:3 * dm], 2, axis=-1)
    kc = _heads_qk(kc)
    vc = _heads_v(vc)
    k_all = jnp.concatenate([kc, kx], axis=3)
    v_all = jnp.concatenate([vc, vx], axis=2)
    yx = _diff_out(_diff_attend(qx, k_all, v_all, lam), gx, w_out, subln_g, lam_init)
    if ctx_out:
        yc = _diff_out(_diff_attend(_heads_qk(qc), kc, vc, lam), gc, w_out, subln_g, lam_init)
    else:
        yc = None
    return yx, yc


def _centred_pool_minus_self(u, w):
    s = u.shape[1]
    lo_off = w // 2
    hi_off = w - 1 - lo_off
    uf = u.astype(jnp.float32)
    cs = jnp.cumsum(uf, axis=1)
    cs = jnp.pad(cs, ((0, 0), (1 + lo_off, 0), (0, 0)))
    cs = jnp.pad(cs, ((0, 0), (0, hi_off), (0, 0)), mode='edge')
    win = cs[:, w:w + s] - cs[:, :s]
    t = jnp.arange(s)
    cnt = (jnp.minimum(t + hi_off + 1, s) - jnp.maximum(t - lo_off, 0)).astype(jnp.float32)
    return (win / cnt[None, :, None] - uf).astype(u.dtype)


def _pool_mixer(h, w_in, grp_w, ch_scale, w_out):
    b, s, _ = h.shape
    u, g = jnp.split(h @ w_in, 2, axis=-1)
    groups = jnp.split(u, POOL_GROUPS, axis=-1)
    pooled = jnp.stack([_centred_pool_minus_self(ug, w) for ug, w in zip(groups, POOL_WINDOWS)], axis=-2)
    mixed = jnp.einsum('bsgc,gce->bsge', pooled, grp_w).reshape(b, s, POOL_WIDTH)
    return ((mixed * ch_scale) * jax.nn.silu(g)) @ w_out


def setup_inputs(seed: int = 0) -> dict:
    key = jax.random.key(seed)
    ks = jax.random.split(key, 20)
    D = D_MODEL
    f = jnp.float32
    nrm = lambda k, shape, s: jax.random.normal(k, shape, f) * s
    return {
        'x': nrm(ks[0], (BATCH, SEQ, D), 1.0),
        'c': nrm(ks[1], (BATCH, D), 1.0),
        'ctx': nrm(ks[2], (BATCH, CTX_LEN, D), 1.0),
        'c_ctx': nrm(ks[3], (D,), 1.0),
        'mod_w': nrm(ks[4], (DEPTH, D, 3 * D), 0.5 * D ** -0.5),
        'mod_b': nrm(ks[5], (DEPTH, 3 * D), 0.02),
        'ln_g': 1.0 + nrm(ks[6], (DEPTH, D), 0.02),
        'ln_b': nrm(ks[7], (DEPTH, D), 0.02),
        'attn_w_in': nrm(ks[8], (N_ATTN_LAYERS, D, 4 * ATTN_WIDTH), D ** -0.5),
        'attn_w_out': nrm(ks[9], (N_ATTN_LAYERS, ATTN_WIDTH, D), DEEPNORM_BETA * ATTN_WIDTH ** -0.5),
        'attn_lq1': nrm(ks[10], (N_ATTN_LAYERS, DIFF_HEAD_DIM), 0.1),
        'attn_lk1': nrm(ks[11], (N_ATTN_LAYERS, DIFF_HEAD_DIM), 0.1),
        'attn_lq2': nrm(ks[12], (N_ATTN_LAYERS, DIFF_HEAD_DIM), 0.1),
        'attn_lk2': nrm(ks[13], (N_ATTN_LAYERS, DIFF_HEAD_DIM), 0.1),
        'attn_subln_g': 1.0 + nrm(ks[14], (N_ATTN_LAYERS, DIFF_V_DIM), 0.02),
        'pool_w_in': nrm(ks[15], (N_POOL_LAYERS, D, 2 * POOL_WIDTH), D ** -0.5),
        'pool_grp_w': nrm(ks[16], (N_POOL_LAYERS, POOL_GROUPS, POOL_GROUP_DIM, POOL_GROUP_DIM), POOL_GROUP_DIM ** -0.5),
        'pool_scale': 1.0 + nrm(ks[17], (N_POOL_LAYERS, POOL_WIDTH), 0.02),
        'pool_w_out': nrm(ks[18], (N_POOL_LAYERS, POOL_WIDTH, D), DEEPNORM_BETA * POOL_WIDTH ** -0.5),
    }


def reference(x, c, ctx, c_ctx, mod_w, mod_b, ln_g, ln_b, attn_w_in, attn_w_out, attn_lq1, attn_lk1,
              attn_lq2, attn_lk2, attn_subln_g, pool_w_in, pool_grp_w, pool_scale, pool_w_out):
    s = x.shape[1]
    rows = s // GRID_W
    t_row = jnp.repeat(jnp.arange(rows, dtype=jnp.float32), GRID_W)
    t_col = jnp.tile(jnp.arange(GRID_W, dtype=jnp.float32), rows)
    inv_freq = ROPE_THETA ** (-jnp.arange(ROPE_AXIS_PAIRS, dtype=jnp.float32) / ROPE_AXIS_PAIRS)
    ang_r = t_row[:, None] * inv_freq
    ang_c = t_col[:, None] * inv_freq
    rope = (jnp.cos(ang_r).astype(x.dtype), jnp.sin(ang_r).astype(x.dtype),
            jnp.cos(ang_c).astype(x.dtype), jnp.sin(ang_c).astype(x.dtype))

    silu_c = jax.nn.silu(c)
    silu_cc = jax.nn.silu(c_ctx)

    for i in range(DEPTH):
        is_attn = (i % N_MIXERS) == 0
        ctx_out = any((j % N_MIXERS) == 0 for j in range(i + 1, DEPTH))
        shift, scale, gate = jnp.split(silu_c @ mod_w[i] + mod_b[i], 3, axis=-1)
        hx = x * (1.0 + scale[:, None, :]) + shift[:, None, :]
        if is_attn or ctx_out:
            shift_c, scale_c, gate_c = jnp.split(silu_cc @ mod_w[i] + mod_b[i], 3, axis=-1)
            hc = ctx * (1.0 + scale_c) + shift_c
        if is_attn:
            a = i // N_MIXERS
            lam_init = 0.8 - 0.6 * math.exp(-0.3 * i)
            yx, yc = _attn_mixer(hx, hc, attn_w_in[a], attn_w_out[a], attn_lq1[a], attn_lk1[a],
                                 attn_lq2[a], attn_lk2[a], attn_subln_g[a], lam_init, rope, ctx_out)
        else:
            p = i // N_MIXERS
            yx = _pool_mixer(hx, pool_w_in[p], pool_grp_w[p], pool_scale[p], pool_w_out[p])
            yc = _pool_mixer(hc, pool_w_in[p], pool_grp_w[p], pool_scale[p], pool_w_out[p]) if ctx_out else None
        x = _layer_norm(DEEPNORM_ALPHA * x + gate[:, None, :] * yx, ln_g[i], ln_b[i])
        if ctx_out:
            ctx = _layer_norm(DEEPNORM_ALPHA * ctx + gate_c * yc, ln_g[i], ln_b[i])
    return x
```

```python
import functools
import math

import jax
import jax.numpy as jnp
from jax import lax
from jax.experimental import pallas as pl
from jax.experimental.pallas import tpu as pltpu

N_HEADS = 8
HEAD_DIM = 128
V_DIM = 2 * HEAD_DIM
GRID_W = 64
ROPE_THETA = 10000.0
ROPE_PAIRS = HEAD_DIM // 4
POOL_WINDOWS = (2, 4, 8, 16)
POOL_HALO = 8
LN_EPS = 1e-5
SUBLN_EPS = 1e-5

LANES = 128
SUBLANES = 8
VMEM_LIMIT_BYTES = 56 * 1024 * 1024

PROJ_ROWS = 512
PROJ_COLS = 2048
ATTN_Q_ROWS = 512
ATTN_K_ROWS = 1024
MOD_COLS = 1024
MOD_ROWS = SUBLANES

LOG2E = 1.4426950408889634


def _silu(v):
    return v / (1.0 + jnp.exp(-v))


def _params(*semantics):
    return pltpu.CompilerParams(dimension_semantics=semantics, vmem_limit_bytes=VMEM_LIMIT_BYTES)


def _mod_kernel(c_ref, w_ref, b_ref, o_ref):
    s = _silu(c_ref[...])
    o_ref[...] = jnp.dot(s, w_ref[...], precision=lax.Precision.HIGHEST,
                         preferred_element_type=jnp.float32) + b_ref[...]


def _modulation(cvec, mod_w, mod_b):
    depth, d, n = mod_w.shape
    tn = min(MOD_COLS, n)
    return pl.pallas_call(
        _mod_kernel,
        grid=(depth, n // tn),
        in_specs=[
            pl.BlockSpec((MOD_ROWS, d), lambda l, j: (0, 0)),
            pl.BlockSpec((None, d, tn), lambda l, j: (l, 0, j)),
            pl.BlockSpec((None, 1, tn), lambda l, j: (l, 0, j)),
        ],
        out_specs=pl.BlockSpec((None, MOD_ROWS, tn), lambda l, j: (l, 0, j)),
        out_shape=jax.ShapeDtypeStruct((depth, MOD_ROWS, n), jnp.float32),
        compiler_params=_params("parallel", "parallel"),
        name="modulation",
    )(cvec, mod_w, mod_b.reshape(depth, 1, n))


def _proj_kernel(x_ref, sh_ref, sc_ref, w_ref, *rest, n_rope):
    o_ref = rest[-1]
    h = (x_ref[...] * (1.0 + sc_ref[...]) + sh_ref[...]).astype(jnp.bfloat16)
    acc = jnp.dot(h, w_ref[...], preferred_element_type=jnp.float32)
    if n_rope == 0:
        o_ref[...] = acc.astype(o_ref.dtype)
        return
    c_ref, sa_ref, sb_ref = rest[:3]
    n = pl.program_id(0)

    @pl.when(n < n_rope)
    def _():
        c, sa, sb = c_ref[...], sa_ref[...], sb_ref[...]
        for gi in range(acc.shape[1] // LANES):
            t = acc[:, gi * LANES:(gi + 1) * LANES]
            r = t * c + pltpu.roll(t, LANES - ROPE_PAIRS, 1) * sa + pltpu.roll(t, ROPE_PAIRS, 1) * sb
            o_ref[:, gi * LANES:(gi + 1) * LANES] = r.astype(o_ref.dtype)

    @pl.when(n >= n_rope)
    def _():
        o_ref[...] = acc.astype(o_ref.dtype)


def _project_qkv(x, shift, scale, w_qkv, tables):
    b, s, d = x.shape
    tn = PROJ_COLS
    nt = w_qkv.shape[1] // tn
    tm = min(PROJ_ROWS, s)
    tab_spec = pl.BlockSpec((None, tm, LANES), lambda n, bb, i: (jnp.minimum(n, 1), i, 0))
    return pl.pallas_call(
        functools.partial(_proj_kernel, n_rope=2),
        grid=(nt, b, s // tm),
        in_specs=[
            pl.BlockSpec((None, tm, d), lambda n, bb, i: (bb, i, 0)),
            pl.BlockSpec((None, 1, d), lambda n, bb, i: (bb, 0, 0)),
            pl.BlockSpec((None, 1, d), lambda n, bb, i: (bb, 0, 0)),
            pl.BlockSpec((d, tn), lambda n, bb, i: (0, n)),
            tab_spec, tab_spec, tab_spec,
        ],
        out_specs=pl.BlockSpec((None, None, tm, tn), lambda n, bb, i: (n, bb, i, 0)),
        out_shape=jax.ShapeDtypeStruct((nt, b, s, tn), jnp.bfloat16),
        compiler_params=_params("arbitrary", "arbitrary", "arbitrary"),
        name="project_qkv",
    )(x, shift, scale, w_qkv, *tables)


def _project_f32(x, shift, scale, w):
    b, s, d = x.shape
    tn = PROJ_COLS
    nt = w.shape[1] // tn
    tm = min(PROJ_ROWS, s)
    return pl.pallas_call(
        functools.partial(_proj_kernel, n_rope=0),
        grid=(nt, b, s // tm),
        in_specs=[
            pl.BlockSpec((None, tm, d), lambda n, bb, i: (bb, i, 0)),
            pl.BlockSpec((None, 1, d), lambda n, bb, i: (bb, 0, 0)),
            pl.BlockSpec((None, 1, d), lambda n, bb, i: (bb, 0, 0)),
            pl.BlockSpec((d, tn), lambda n, bb, i: (0, n)),
        ],
        out_specs=pl.BlockSpec((None, tm, tn), lambda n, bb, i: (bb, i, n)),
        out_shape=jax.ShapeDtypeStruct((b, s, nt * tn), jnp.float32),
        compiler_params=_params("arbitrary", "arbitrary", "arbitrary"),
        name="project_f32",
    )(x, shift, scale, w)


def _attn_kernel(lq1_ref, lk1_ref, lq2_ref, lk2_ref, subg_ref, q_ref, kc_ref, vc_ref, *rest,
                 lam_init, has_x_keys):
    if has_x_keys:
        k_ref, v_ref, g_ref, o_ref, m_sc, l_sc, acc_sc = rest
    else:
        g_ref, o_ref, m_sc, l_sc, acc_sc = rest
    j = pl.program_id(3)

    @pl.when(j == 0)
    def _():
        m_sc[...] = jnp.full(m_sc.shape, -jnp.inf, jnp.float32)
        l_sc[...] = jnp.zeros(l_sc.shape, jnp.float32)
        acc_sc[...] = jnp.zeros(acc_sc.shape, jnp.float32)

    def accumulate(kb_ref, vb_ref):
        v = vb_ref[...]
        for comp in range(2):
            cols = slice(comp * HEAD_DIM, (comp + 1) * HEAD_DIM)
            s = lax.dot_general(q_ref[:, cols], kb_ref[:, cols], (((1,), (1,)), ((), ())),
                                preferred_element_type=jnp.float32)
            m_prev = m_sc[comp]
            m_new = jnp.maximum(m_prev, jnp.max(s, axis=-1, keepdims=True))
            alpha = jnp.exp2(m_prev - m_new)
            p = jnp.exp2(s - m_new)
            part = p[:, 0:LANES]
            for ci in range(1, p.shape[1] // LANES):
                part = part + p[:, ci * LANES:(ci + 1) * LANES]
            l_sc[comp] = alpha * l_sc[comp] + part
            acc_sc[comp] = alpha * acc_sc[comp] + jnp.dot(p.astype(jnp.bfloat16), v,
                                                          preferred_element_type=jnp.float32)
            m_sc[comp] = m_new

    @pl.when(j == 0)
    def _():
        accumulate(kc_ref, vc_ref)

    if has_x_keys:
        @pl.when(j > 0)
        def _():
            accumulate(k_ref, v_ref)

    @pl.when(j == pl.num_programs(3) - 1)
    def _():
        lam = (jnp.exp(jnp.sum(lq1_ref[...] * lk1_ref[...], axis=-1, keepdims=True))
               - jnp.exp(jnp.sum(lq2_ref[...] * lk2_ref[...], axis=-1, keepdims=True)) + lam_init)
        l1 = jnp.sum(l_sc[0], axis=-1, keepdims=True)
        l2 = jnp.sum(l_sc[1], axis=-1, keepdims=True)
        o = acc_sc[0] / l1 - lam * (acc_sc[1] / l2)
        o = o * lax.rsqrt(jnp.mean(o * o, axis=-1, keepdims=True) + SUBLN_EPS)
        o = o * subg_ref[...] * (1.0 - lam_init)
        o_ref[...] = (o * _silu(g_ref[...])).astype(o_ref.dtype)


def _attention(qkv_q, qkv_ctx, qkv_x, gate_path, lams, subln_g, lam_init):
    _, b, sq, width = qkv_q.shape
    ctx_len = qkv_ctx.shape[2]
    tq = min(ATTN_Q_ROWS, sq)
    has_x_keys = qkv_x is not None
    small = pl.BlockSpec((1, HEAD_DIM), lambda bb, h, i, j: (0, 0))
    in_specs = [
        small, small, small, small,
        pl.BlockSpec((1, V_DIM), lambda bb, h, i, j: (0, 0)),
        pl.BlockSpec((None, None, tq, V_DIM), lambda bb, h, i, j: (0, bb, i, h)),
        pl.BlockSpec((None, None, ctx_len, V_DIM), lambda bb, h, i, j: (1, bb, 0, h)),
        pl.BlockSpec((None, None, ctx_len, V_DIM), lambda bb, h, i, j: (2, bb, 0, h)),
    ]
    args = [*lams, subln_g, qkv_q, qkv_ctx, qkv_ctx]
    steps = 1
    if has_x_keys:
        sk = qkv_x.shape[2]
        tk = min(ATTN_K_ROWS, sk)
        steps += sk // tk
        in_specs += [
            pl.BlockSpec((None, None, tk, V_DIM), lambda bb, h, i, j: (1, bb, jnp.maximum(j - 1, 0), h)),
            pl.BlockSpec((None, None, tk, V_DIM), lambda bb, h, i, j: (2, bb, jnp.maximum(j - 1, 0), h)),
        ]
        args += [qkv_x, qkv_x]
    in_specs.append(pl.BlockSpec((None, tq, V_DIM), lambda bb, h, i, j: (bb, i, h)))
    args.append(gate_path)
    return pl.pallas_call(
        functools.partial(_attn_kernel, lam_init=lam_init, has_x_keys=has_x_keys),
        grid=(b, N_HEADS, sq // tq, steps),
        in_specs=in_specs,
        out_specs=pl.BlockSpec((None, tq, V_DIM), lambda bb, h, i, j: (bb, i, h)),
        out_shape=jax.ShapeDtypeStruct((b, sq, width), jnp.bfloat16),
        scratch_shapes=[
            pltpu.VMEM((2, tq, 1), jnp.float32),
            pltpu.VMEM((2, tq, LANES), jnp.float32),
            pltpu.VMEM((2, tq, V_DIM), jnp.float32),
        ],
        compiler_params=_params("parallel", "parallel", "parallel", "arbitrary"),
        name="diff_attention",
    )(*args)


def _residual_layer_norm(x, gate, y, ln_g, ln_b, alpha):
    z = alpha * x + gate * y
    mu = jnp.mean(z, axis=-1, keepdims=True)
    zc = z - mu
    var = jnp.mean(zc * zc, axis=-1, keepdims=True)
    return zc * lax.rsqrt(var + LN_EPS) * ln_g + ln_b


def _out_kernel(y_ref, x_ref, gate_ref, w_ref, lg_ref, lb_ref, o_ref, *, alpha):
    y = jnp.dot(y_ref[...], w_ref[...], preferred_element_type=jnp.float32)
    o_ref[...] = _residual_layer_norm(x_ref[...], gate_ref[...], y, lg_ref[...], lb_ref[...], alpha)


def _output_layer_norm(y, x, gate, w_out, ln_g, ln_b, alpha):
    b, s, d = x.shape
    tm = min(PROJ_ROWS, s)
    vec = pl.BlockSpec((1, d), lambda bb, i: (0, 0))
    return pl.pallas_call(
        functools.partial(_out_kernel, alpha=alpha),
        grid=(b, s // tm),
        in_specs=[
            pl.BlockSpec((None, tm, y.shape[2]), lambda bb, i: (bb, i, 0)),
            pl.BlockSpec((None, tm, d), lambda bb, i: (bb, i, 0)),
            pl.BlockSpec((None, 1, d), lambda bb, i: (bb, 0, 0)),
            pl.BlockSpec(w_out.shape, lambda bb, i: (0, 0)),
            vec, vec,
        ],
        out_specs=pl.BlockSpec((None, tm, d), lambda bb, i: (bb, i, 0)),
        out_shape=jax.ShapeDtypeStruct((b, s, d), jnp.float32),
        compiler_params=_params("parallel", "parallel"),
        name="output_layer_norm",
    )(y, x, gate, w_out, ln_g, ln_b)


def _pool_kernel(u_ref, up_ref, un_ref, g_ref, x_ref, gate_ref, gw_ref, cs_ref, w_ref, lg_ref, lb_ref,
                 o_ref, ubuf, mix, *, seq, alpha):
    i = pl.program_id(1)
    tm = u_ref.shape[0]
    gdim = u_ref.shape[1] // len(POOL_WINDOWS)
    ubuf[0:POOL_HALO, :] = jnp.where(i > 0, up_ref[...], 0.0)
    ubuf[POOL_HALO:POOL_HALO + tm, :] = u_ref[...]
    ubuf[POOL_HALO + tm:2 * POOL_HALO + tm, :] = jnp.where(i < pl.num_programs(1) - 1, un_ref[...], 0.0)
    t = i * tm + lax.broadcasted_iota(jnp.int32, (tm, 1), 0)
    for gi, w in enumerate(POOL_WINDOWS):
        lo = w // 2
        hi = w - 1 - lo
        cols = slice(gi * gdim, (gi + 1) * gdim)
        win = ubuf[POOL_HALO - lo:POOL_HALO - lo + tm, cols]
        for off in range(-lo + 1, hi + 1):
            win = win + ubuf[POOL_HALO + off:POOL_HALO + off + tm, cols]
        cnt = (jnp.minimum(t + hi + 1, seq) - jnp.maximum(t - lo, 0)).astype(jnp.float32)
        pooled = win / cnt - u_ref[:, cols]
        mix[:, cols] = jnp.dot(pooled.astype(jnp.bfloat16), gw_ref[gi], preferred_element_type=jnp.float32)
    gated = ((mix[...] * cs_ref[...]) * _silu(g_ref[...])).astype(jnp.bfloat16)
    y = jnp.dot(gated, w_ref[...], preferred_element_type=jnp.float32)
    o_ref[...] = _residual_layer_norm(x_ref[...], gate_ref[...], y, lg_ref[...], lb_ref[...], alpha)


def _pool_mix_output(ug, x, gate, grp_w, ch_scale, w_out, ln_g, ln_b, alpha):
    b, s, d = x.shape
    width = ug.shape[2] // 2
    tm = min(PROJ_ROWS, s)
    halo_blocks = tm // POOL_HALO
    last_halo = s // POOL_HALO - 1
    vec = pl.BlockSpec((1, d), lambda bb, i: (0, 0))
    return pl.pallas_call(
        functools.partial(_pool_kernel, seq=s, alpha=alpha),
        grid=(b, s // tm),
        in_specs=[
            pl.BlockSpec((None, tm, width), lambda bb, i: (bb, i, 0)),
            pl.BlockSpec((None, POOL_HALO, width), lambda bb, i: (bb, jnp.maximum(i * halo_blocks - 1, 0), 0)),
            pl.BlockSpec((None, POOL_HALO, width),
                         lambda bb, i: (bb, jnp.minimum((i + 1) * halo_blocks, last_halo), 0)),
            pl.BlockSpec((None, tm, width), lambda bb, i: (bb, i, 1)),
            pl.BlockSpec((None, tm, d), lambda bb, i: (bb, i, 0)),
            pl.BlockSpec((None, 1, d), lambda bb, i: (bb, 0, 0)),
            pl.BlockSpec(grp_w.shape, lambda bb, i: (0, 0, 0)),
            pl.BlockSpec((1, width), lambda bb, i: (0, 0)),
            pl.BlockSpec(w_out.shape, lambda bb, i: (0, 0)),
            vec, vec,
        ],
        out_specs=pl.BlockSpec((None, tm, d), lambda bb, i: (bb, i, 0)),
        out_shape=jax.ShapeDtypeStruct((b, s, d), jnp.float32),
        scratch_shapes=[
            pltpu.VMEM((tm + 2 * POOL_HALO, width), jnp.float32),
            pltpu.VMEM((tm, width), jnp.float32),
        ],
        compiler_params=_params("parallel", "parallel"),
        name="pool_mix_output",
    )(ug, ug, ug, ug, x, gate, grp_w, ch_scale, w_out, ln_g, ln_b)


def _rope_tables(s, q_scale):
    t = jnp.arange(s)
    t_row = (t // GRID_W).astype(jnp.float32)
    t_col = (t % GRID_W).astype(jnp.float32)
    inv_freq = ROPE_THETA ** (-jnp.arange(ROPE_PAIRS, dtype=jnp.float32) / ROPE_PAIRS)
    ang_r = t_row[:, None] * inv_freq
    ang_c = t_col[:, None] * inv_freq
    cr, sr, cc, sc = jnp.cos(ang_r), jnp.sin(ang_r), jnp.cos(ang_c), jnp.sin(ang_c)
    z = jnp.zeros_like(sr)
    c = jnp.concatenate([cr, cr, cc, cc], axis=-1)
    sa = jnp.concatenate([-sr, z, -sc, z], axis=-1)
    sb = jnp.concatenate([z, sr, z, sc], axis=-1)
    return tuple(jnp.stack([tab * q_scale, tab]) for tab in (c, sa, sb))


def _identity_tables(s, q_scale):
    one = jnp.ones((s, LANES), jnp.float32)
    zero = jnp.zeros((2, s, LANES), jnp.float32)
    return jnp.stack([one * q_scale, one]), zero, zero


def kernel(x, c, ctx, c_ctx, mod_w, mod_b, ln_g, ln_b, attn_w_in, attn_w_out, attn_lq1, attn_lk1, attn_lq2,
           attn_lk2, attn_subln_g, pool_w_in, pool_grp_w, pool_scale, pool_w_out):
    b, s, d = x.shape
    ctx_len = ctx.shape[1]
    depth = mod_w.shape[0]
    assert d == N_HEADS * V_DIM and b + 1 <= MOD_ROWS
    assert s % PROJ_ROWS == 0 or s < PROJ_ROWS
    alpha = (2 * depth) ** 0.25
    q_scale = LOG2E / math.sqrt(HEAD_DIM)
    bf16 = jnp.bfloat16

    cvec = jnp.concatenate([c, c_ctx[None, :], jnp.zeros((MOD_ROWS - b - 1, d), jnp.float32)], axis=0)
    mod = _modulation(cvec, mod_w, mod_b)
    tabs_x = _rope_tables(s, q_scale)
    tabs_c = _identity_tables(ctx_len, q_scale)

    for i in range(depth):
        is_attn = i % 2 == 0
        ctx_out = any(j % 2 == 0 for j in range(i + 1, depth))
        shift, scale, gate = (mod[i, :, k * d:(k + 1) * d] for k in range(3))
        sh_x, sc_x, gt_x = (v[:b, None, :] for v in (shift, scale, gate))
        sh_c, sc_c, gt_c = (jnp.broadcast_to(v[b][None, None, :], (b, 1, d)) for v in (shift, scale, gate))
        lg, lb = ln_g[i][None, :], ln_b[i][None, :]
        if is_attn:
            a = i // 2
            lam_init = 0.8 - 0.6 * math.exp(-0.3 * i)
            w_in = attn_w_in[a].astype(bf16)
            w_qkv, w_gate = w_in[:, :3 * d], w_in[:, 3 * d:]
            w_out = attn_w_out[a].astype(bf16)
            lams = [v[a][None, :] for v in (attn_lq1, attn_lk1, attn_lq2, attn_lk2)]
            subg = attn_subln_g[a][None, :]
            qkv_x = _project_qkv(x, sh_x, sc_x, w_qkv, tabs_x)
            qkv_c = _project_qkv(ctx, sh_c, sc_c, w_qkv, tabs_c)
            g_x = _project_f32(x, sh_x, sc_x, w_gate)
            y_x = _attention(qkv_x, qkv_c, qkv_x, g_x, lams, subg, lam_init)
            if ctx_out:
                g_c = _project_f32(ctx, sh_c, sc_c, w_gate)
                y_c = _attention(qkv_c, qkv_c, None, g_c, lams, subg, lam_init)
                ctx = _output_layer_norm(y_c, ctx, gt_c, w_out, lg, lb, alpha)
            x = _output_layer_norm(y_x, x, gt_x, w_out, lg, lb, alpha)
        else:
            p = i // 2
            w_in = pool_w_in[p].astype(bf16)
            grp_w = pool_grp_w[p].astype(bf16)
            w_out = pool_w_out[p].astype(bf16)
            cs = pool_scale[p][None, :]
            ug_x = _project_f32(x, sh_x, sc_x, w_in)
            if ctx_out:
                ug_c = _project_f32(ctx, sh_c, sc_c, w_in)
                ctx = _pool_mix_output(ug_c, ctx, gt_c, grp_w, cs, w_out, lg, lb, alpha)
            x = _pool_mix_output(ug_x, x, gt_x, grp_w, cs, w_out, lg, lb, alpha)
    return x
```

```python
import functools
import math

import jax
import jax.numpy as jnp
from jax import lax
from jax.experimental import pallas as pl
from jax.experimental.pallas import tpu as pltpu

N_HEADS = 8
HEAD_DIM = 128
V_DIM = 2 * HEAD_DIM
GRID_W = 64
ROPE_THETA = 10000.0
ROPE_PAIRS = HEAD_DIM // 4
POOL_WINDOWS = (2, 4, 8, 16)
POOL_HALO = 8
LN_EPS = 1e-5
SUBLN_EPS = 1e-5

LANES = 128
SUBLANES = 8
MXU_DEPTH = 256
VMEM_LIMIT_BYTES = 56 * 1024 * 1024

PROJ_ROWS = 512
PROJ_COLS = 2048
ATTN_Q_ROWS = 512
ATTN_K_ROWS = 1280
MOD_COLS = 1024
MOD_ROWS = SUBLANES

LOG2E = 1.4426950408889634


def _silu(v):
    return v / (1.0 + jnp.exp(-v))


def _params(*semantics):
    return pltpu.CompilerParams(dimension_semantics=semantics, vmem_limit_bytes=VMEM_LIMIT_BYTES)


def _mod_kernel(c_ref, w_ref, b_ref, o_ref):
    s = _silu(c_ref[...])
    o_ref[...] = jnp.dot(s, w_ref[...], precision=lax.Precision.HIGHEST,
                         preferred_element_type=jnp.float32) + b_ref[...]


def _modulation(cvec, mod_w, mod_b):
    depth, d, n = mod_w.shape
    tn = min(MOD_COLS, n)
    return pl.pallas_call(
        _mod_kernel,
        grid=(depth, n // tn),
        in_specs=[
            pl.BlockSpec((MOD_ROWS, d), lambda l, j: (0, 0)),
            pl.BlockSpec((None, d, tn), lambda l, j: (l, 0, j)),
            pl.BlockSpec((None, 1, tn), lambda l, j: (l, 0, j)),
        ],
        out_specs=pl.BlockSpec((None, MOD_ROWS, tn), lambda l, j: (l, 0, j)),
        out_shape=jax.ShapeDtypeStruct((depth, MOD_ROWS, n), jnp.float32),
        compiler_params=_params("parallel", "parallel"),
        name="modulation",
    )(cvec, mod_w, mod_b.reshape(depth, 1, n))


def _modulated_matmul(x_ref, sh_ref, sc_ref, w_ref):
    h = (x_ref[...] * (1.0 + sc_ref[...]) + sh_ref[...]).astype(jnp.bfloat16)
    return jnp.dot(h, w_ref[...], preferred_element_type=jnp.float32)


def _proj_f32_kernel(x_ref, sh_ref, sc_ref, w_ref, o_ref):
    o_ref[...] = _modulated_matmul(x_ref, sh_ref, sc_ref, w_ref)


def _proj_qkv_kernel(x_ref, sh_ref, sc_ref, w_ref, c_ref, sa_ref, sb_ref, o_ref):
    acc = _modulated_matmul(x_ref, sh_ref, sc_ref, w_ref)
    n = pl.program_id(0)

    @pl.when(n < 2)
    def _():
        c, sa, sb = c_ref[...], sa_ref[...], sb_ref[...]
        for gi in range(acc.shape[1] // LANES):
            t = acc[:, gi * LANES:(gi + 1) * LANES]
            r = t * c + pltpu.roll(t, LANES - ROPE_PAIRS, 1) * sa + pltpu.roll(t, ROPE_PAIRS, 1) * sb
            head, comp = divmod(gi, V_DIM // LANES)
            o_ref[head, :, comp * LANES:(comp + 1) * LANES] = r.astype(o_ref.dtype)

    @pl.when(n >= 2)
    def _():
        for head in range(o_ref.shape[0]):
            o_ref[head] = acc[:, head * V_DIM:(head + 1) * V_DIM].astype(o_ref.dtype)


def _project_qkv(x, shift, scale, w_qkv, tables):
    b, s, d = x.shape
    tn = PROJ_COLS
    nt = w_qkv.shape[1] // tn
    nh = tn // V_DIM
    tm = min(PROJ_ROWS, s)
    tab_spec = pl.BlockSpec((None, tm, LANES), lambda n, bb, i: (jnp.minimum(n, 1), i, 0))
    return pl.pallas_call(
        _proj_qkv_kernel,
        grid=(nt, b, s // tm),
        in_specs=[
            pl.BlockSpec((None, tm, d), lambda n, bb, i: (bb, i, 0)),
            pl.BlockSpec((None, 1, d), lambda n, bb, i: (bb, 0, 0)),
            pl.BlockSpec((None, 1, d), lambda n, bb, i: (bb, 0, 0)),
            pl.BlockSpec((d, tn), lambda n, bb, i: (0, n)),
            tab_spec, tab_spec, tab_spec,
        ],
        out_specs=pl.BlockSpec((None, None, nh, tm, V_DIM), lambda n, bb, i: (n, bb, 0, i, 0)),
        out_shape=jax.ShapeDtypeStruct((nt, b, nh, s, V_DIM), jnp.bfloat16),
        compiler_params=_params("arbitrary", "arbitrary", "arbitrary"),
        name="project_qkv",
    )(x, shift, scale, w_qkv, *tables)


def _project_f32(x, shift, scale, w):
    b, s, d = x.shape
    tn = PROJ_COLS
    nt = w.shape[1] // tn
    tm = min(PROJ_ROWS, s)
    return pl.pallas_call(
        _proj_f32_kernel,
        grid=(nt, b, s // tm),
        in_specs=[
            pl.BlockSpec((None, tm, d), lambda n, bb, i: (bb, i, 0)),
            pl.BlockSpec((None, 1, d), lambda n, bb, i: (bb, 0, 0)),
            pl.BlockSpec((None, 1, d), lambda n, bb, i: (bb, 0, 0)),
            pl.BlockSpec((d, tn), lambda n, bb, i: (0, n)),
        ],
        out_specs=pl.BlockSpec((None, tm, tn), lambda n, bb, i: (bb, i, n)),
        out_shape=jax.ShapeDtypeStruct((b, s, nt * tn), jnp.float32),
        compiler_params=_params("arbitrary", "arbitrary", "arbitrary"),
        name="project_f32",
    )(x, shift, scale, w)


def _key_block(skv):
    best = MXU_DEPTH
    for cand in range(MXU_DEPTH, min(ATTN_K_ROWS, skv) + 1, MXU_DEPTH):
        if skv % cand == 0:
            best = cand
    return best


def _attn_kernel(lq1_ref, lk1_ref, lq2_ref, lk2_ref, subg_ref, q_ref, k0_ref, kn_ref, v_ref, g_ref, o_ref,
                 s_sc, mp_sc, m_sc, l_sc, acc_sc, *, lam_init):
    j = pl.program_id(3)
    tk = kn_ref.shape[0]

    def scores(kb_ref, slot):
        for comp in range(2):
            cols = slice(comp * HEAD_DIM, (comp + 1) * HEAD_DIM)
            s = lax.dot_general(q_ref[:, cols], kb_ref[:, cols], (((1,), (1,)), ((), ())),
                                preferred_element_type=jnp.float32)
            s_sc[slot, comp] = s
            mp = s[:, 0:LANES]
            for ci in range(1, tk // LANES):
                mp = jnp.maximum(mp, s[:, ci * LANES:(ci + 1) * LANES])
            mp_sc[slot, comp] = mp

    def consume(slot):
        for comp in range(2):
            m_blk = jnp.max(mp_sc[slot, comp], axis=-1, keepdims=True)
            m_prev = m_sc[comp]
            m_new = jnp.maximum(m_prev, m_blk)
            alpha = jnp.exp2(m_prev - m_new)
            m_sc[comp] = m_new
            part = None
            pieces = []
            for ci in range(tk // LANES):
                pk = jnp.exp2(s_sc[slot, comp, :, ci * LANES:(ci + 1) * LANES] - m_new)
                part = pk if part is None else part + pk
                pieces.append(pk.astype(jnp.bfloat16))
            p = jnp.concatenate(pieces, axis=1)
            pv = jnp.dot(p, v_ref[...], preferred_element_type=jnp.float32)
            l_sc[comp] = alpha * l_sc[comp] + part
            acc_sc[comp] = jnp.concatenate([alpha, alpha], axis=1) * acc_sc[comp] + pv

    @pl.when(j == 0)
    def _():
        m_sc[...] = jnp.full(m_sc.shape, -jnp.inf, jnp.float32)
        l_sc[...] = jnp.zeros(l_sc.shape, jnp.float32)
        acc_sc[...] = jnp.zeros(acc_sc.shape, jnp.float32)
        scores(k0_ref, 0)

    for parity in range(2):
        @pl.when(j % 2 == parity)
        def _():
            scores(kn_ref, 1 - parity)
            consume(parity)

    @pl.when(j == pl.num_programs(3) - 1)
    def _():
        lam = (jnp.exp(jnp.sum(lq1_ref[...] * lk1_ref[...], axis=-1, keepdims=True))
               - jnp.exp(jnp.sum(lq2_ref[...] * lk2_ref[...], axis=-1, keepdims=True)) + lam_init)
        l1 = jnp.sum(l_sc[0], axis=-1, keepdims=True)
        l2 = jnp.sum(l_sc[1], axis=-1, keepdims=True)
        o = acc_sc[0] / l1 - lam * (acc_sc[1] / l2)
        o = o * lax.rsqrt(jnp.mean(o * o, axis=-1, keepdims=True) + SUBLN_EPS)
        o = o * subg_ref[...] * (1.0 - lam_init)
        o_ref[...] = (o * _silu(g_ref[...])).astype(o_ref.dtype)


def _attention(qkv, k, v, gate_path, lams, subln_g, lam_init):
    _, b, nh, sq, _ = qkv.shape
    skv = k.shape[2]
    tq = min(ATTN_Q_ROWS, sq)
    tk = _key_block(skv)
    nk = skv // tk
    small = pl.BlockSpec((1, HEAD_DIM), lambda bb, h, i, j: (0, 0))
    return pl.pallas_call(
        functools.partial(_attn_kernel, lam_init=lam_init),
        grid=(b, nh, sq // tq, nk),
        in_specs=[
            small, small, small, small,
            pl.BlockSpec((1, V_DIM), lambda bb, h, i, j: (0, 0)),
            pl.BlockSpec((None, None, None, tq, V_DIM), lambda bb, h, i, j: (0, bb, h, i, 0)),
            pl.BlockSpec((None, None, tk, V_DIM), lambda bb, h, i, j: (bb, h, 0, 0)),
            pl.BlockSpec((None, None, tk, V_DIM), lambda bb, h, i, j: (bb, h, jnp.minimum(j + 1, nk - 1), 0)),
            pl.BlockSpec((None, None, tk, V_DIM), lambda bb, h, i, j: (bb, h, j, 0)),
            pl.BlockSpec((None, tq, V_DIM), lambda bb, h, i, j: (bb, i, h)),
        ],
        out_specs=pl.BlockSpec((None, tq, V_DIM), lambda bb, h, i, j: (bb, i, h)),
        out_shape=jax.ShapeDtypeStruct((b, sq, nh * V_DIM), jnp.bfloat16),
        scratch_shapes=[
            pltpu.VMEM((2, 2, tq, tk), jnp.float32),
            pltpu.VMEM((2, 2, tq, LANES), jnp.float32),
            pltpu.VMEM((2, tq, LANES), jnp.float32),
            pltpu.VMEM((2, tq, LANES), jnp.float32),
            pltpu.VMEM((2, tq, V_DIM), jnp.float32),
        ],
        compiler_params=_params("parallel", "parallel", "parallel", "arbitrary"),
        name="diff_attention",
    )(*lams, subln_g, qkv, k, k, v, gate_path)


def _residual_layer_norm(x, gate, y, ln_g, ln_b, alpha):
    z = alpha * x + gate * y
    mu = jnp.mean(z, axis=-1, keepdims=True)
    zc = z - mu
    var = jnp.mean(zc * zc, axis=-1, keepdims=True)
    return zc * lax.rsqrt(var + LN_EPS) * ln_g + ln_b


def _out_kernel(y_ref, x_ref, gate_ref, w_ref, lg_ref, lb_ref, o_ref, *, alpha):
    y = jnp.dot(y_ref[...], w_ref[...], preferred_element_type=jnp.float32)
    o_ref[...] = _residual_layer_norm(x_ref[...], gate_ref[...], y, lg_ref[...], lb_ref[...], alpha)


def _output_layer_norm(y, x, gate, w_out, ln_g, ln_b, alpha):
    b, s, d = x.shape
    tm = min(PROJ_ROWS, s)
    vec = pl.BlockSpec((1, d), lambda bb, i: (0, 0))
    return pl.pallas_call(
        functools.partial(_out_kernel, alpha=alpha),
        grid=(b, s // tm),
        in_specs=[
            pl.BlockSpec((None, tm, y.shape[2]), lambda bb, i: (bb, i, 0)),
            pl.BlockSpec((None, tm, d), lambda bb, i: (bb, i, 0)),
            pl.BlockSpec((None, 1, d), lambda bb, i: (bb, 0, 0)),
            pl.BlockSpec(w_out.shape, lambda bb, i: (0, 0)),
            vec, vec,
        ],
        out_specs=pl.BlockSpec((None, tm, d), lambda bb, i: (bb, i, 0)),
        out_shape=jax.ShapeDtypeStruct((b, s, d), jnp.float32),
        compiler_params=_params("parallel", "parallel"),
        name="output_layer_norm",
    )(y, x, gate, w_out, ln_g, ln_b)


def _pool_kernel(u_ref, up_ref, un_ref, g_ref, x_ref, gate_ref, gw_ref, cs_ref, w_ref, lg_ref, lb_ref,
                 o_ref, ubuf, mix, *, seq, alpha):
    i = pl.program_id(1)
    tm = u_ref.shape[0]
    gdim = u_ref.shape[1] // len(POOL_WINDOWS)
    ubuf[0:POOL_HALO, :] = jnp.where(i > 0, up_ref[...], 0.0)
    ubuf[POOL_HALO:POOL_HALO + tm, :] = u_ref[...]
    ubuf[POOL_HALO + tm:2 * POOL_HALO + tm, :] = jnp.where(i < pl.num_programs(1) - 1, un_ref[...], 0.0)
    t = i * tm + lax.broadcasted_iota(jnp.int32, (tm, 1), 0)
    for gi, w in enumerate(POOL_WINDOWS):
        lo = w // 2
        hi = w - 1 - lo
        cols = slice(gi * gdim, (gi + 1) * gdim)
        win = ubuf[POOL_HALO - lo:POOL_HALO - lo + tm, cols]
        for off in range(-lo + 1, hi + 1):
            win = win + ubuf[POOL_HALO + off:POOL_HALO + off + tm, cols]
        cnt = (jnp.minimum(t + hi + 1, seq) - jnp.maximum(t - lo, 0)).astype(jnp.float32)
        pooled = win / cnt - u_ref[:, cols]
        mix[:, cols] = jnp.dot(pooled.astype(jnp.bfloat16), gw_ref[gi], preferred_element_type=jnp.float32)
    gated = ((mix[...] * cs_ref[...]) * _silu(g_ref[...])).astype(jnp.bfloat16)
    y = jnp.dot(gated, w_ref[...], preferred_element_type=jnp.float32)
    o_ref[...] = _residual_layer_norm(x_ref[...], gate_ref[...], y, lg_ref[...], lb_ref[...], alpha)


def _pool_mix_output(ug, x, gate, grp_w, ch_scale, w_out, ln_g, ln_b, alpha):
    b, s, d = x.shape
    width = ug.shape[2] // 2
    tm = min(PROJ_ROWS, s)
    halo_blocks = tm // POOL_HALO
    last_halo = s // POOL_HALO - 1
    vec = pl.BlockSpec((1, d), lambda bb, i: (0, 0))
    return pl.pallas_call(
        functools.partial(_pool_kernel, seq=s, alpha=alpha),
        grid=(b, s // tm),
        in_specs=[
            pl.BlockSpec((None, tm, width), lambda bb, i: (bb, i, 0)),
            pl.BlockSpec((None, POOL_HALO, width), lambda bb, i: (bb, jnp.maximum(i * halo_blocks - 1, 0), 0)),
            pl.BlockSpec((None, POOL_HALO, width),
                         lambda bb, i: (bb, jnp.minimum((i + 1) * halo_blocks, last_halo), 0)),
            pl.BlockSpec((None, tm, width), lambda bb, i: (bb, i, 1)),
            pl.BlockSpec((None, tm, d), lambda bb, i: (bb, i, 0)),
            pl.BlockSpec((None, 1, d), lambda bb, i: (bb, 0, 0)),
            pl.BlockSpec(grp_w.shape, lambda bb, i: (0, 0, 0)),
            pl.BlockSpec((1, width), lambda bb, i: (0, 0)),
            pl.BlockSpec(w_out.shape, lambda bb, i: (0, 0)),
            vec, vec,
        ],
        out_specs=pl.BlockSpec((None, tm, d), lambda bb, i: (bb, i, 0)),
        out_shape=jax.ShapeDtypeStruct((b, s, d), jnp.float32),
        scratch_shapes=[
            pltpu.VMEM((tm + 2 * POOL_HALO, width), jnp.float32),
            pltpu.VMEM((tm, width), jnp.float32),
        ],
        compiler_params=_params("parallel", "parallel"),
        name="pool_mix_output",
    )(ug, ug, ug, ug, x, gate, grp_w, ch_scale, w_out, ln_g, ln_b)


def _rope_tables(s, q_scale):
    t = jnp.arange(s)
    t_row = (t // GRID_W).astype(jnp.float32)
    t_col = (t % GRID_W).astype(jnp.float32)
    inv_freq = ROPE_THETA ** (-jnp.arange(ROPE_PAIRS, dtype=jnp.float32) / ROPE_PAIRS)
    ang_r = t_row[:, None] * inv_freq
    ang_c = t_col[:, None] * inv_freq
    cr, sr, cc, sc = jnp.cos(ang_r), jnp.sin(ang_r), jnp.cos(ang_c), jnp.sin(ang_c)
    z = jnp.zeros_like(sr)
    c = jnp.concatenate([cr, cr, cc, cc], axis=-1)
    sa = jnp.concatenate([-sr, z, -sc, z], axis=-1)
    sb = jnp.concatenate([z, sr, z, sc], axis=-1)
    return tuple(jnp.stack([tab * q_scale, tab]) for tab in (c, sa, sb))


def _identity_tables(s, q_scale):
    one = jnp.ones((s, LANES), jnp.float32)
    zero = jnp.zeros((2, s, LANES), jnp.float32)
    return jnp.stack([one * q_scale, one]), zero, zero


def kernel(x, c, ctx, c_ctx, mod_w, mod_b, ln_g, ln_b, attn_w_in, attn_w_out, attn_lq1, attn_lk1, attn_lq2,
           attn_lk2, attn_subln_g, pool_w_in, pool_grp_w, pool_scale, pool_w_out):
    b, s, d = x.shape
    ctx_len = ctx.shape[1]
    depth = mod_w.shape[0]
    assert d == N_HEADS * V_DIM and b + 1 <= MOD_ROWS
    assert s % PROJ_ROWS == 0 or s < PROJ_ROWS
    alpha = (2 * depth) ** 0.25
    q_scale = LOG2E / math.sqrt(HEAD_DIM)
    bf16 = jnp.bfloat16

    cvec = jnp.concatenate([c, c_ctx[None, :], jnp.zeros((MOD_ROWS - b - 1, d), jnp.float32)], axis=0)
    mod = _modulation(cvec, mod_w, mod_b)
    tabs_x = _rope_tables(s, q_scale)
    tabs_c = _identity_tables(ctx_len, q_scale)

    for i in range(depth):
        is_attn = i % 2 == 0
        ctx_out = any(j % 2 == 0 for j in range(i + 1, depth))
        shift, scale, gate = (mod[i, :, k * d:(k + 1) * d] for k in range(3))
        sh_x, sc_x, gt_x = (v[:b, None, :] for v in (shift, scale, gate))
        sh_c, sc_c, gt_c = (jnp.broadcast_to(v[b][None, None, :], (b, 1, d)) for v in (shift, scale, gate))
        lg, lb = ln_g[i][None, :], ln_b[i][None, :]
        if is_attn:
            a = i // 2
            lam_init = 0.8 - 0.6 * math.exp(-0.3 * i)
            w_in = attn_w_in[a].astype(bf16)
            w_qkv, w_gate = w_in[:, :3 * d], w_in[:, 3 * d:]
            w_out = attn_w_out[a].astype(bf16)
            lams = [v[a][None, :] for v in (attn_lq1, attn_lk1, attn_lq2, attn_lk2)]
            subg = attn_subln_g[a][None, :]
            qkv_x = _project_qkv(x, sh_x, sc_x, w_qkv, tabs_x)
            qkv_c = _project_qkv(ctx, sh_c, sc_c, w_qkv, tabs_c)
            g_x = _project_f32(x, sh_x, sc_x, w_gate)
            k_all = jnp.concatenate([qkv_c[1], qkv_x[1]], axis=2)
            v_all = jnp.concatenate([qkv_c[2], qkv_x[2]], axis=2)
            y_x = _attention(qkv_x, k_all, v_all, g_x, lams, subg, lam_init)
            if ctx_out:
                g_c = _project_f32(ctx, sh_c, sc_c, w_gate)
                y_c = _attention(qkv_c, qkv_c[1], qkv_c[2], g_c, lams, subg, lam_init)
                ctx = _output_layer_norm(y_c, ctx, gt_c, w_out, lg, lb, alpha)
            x = _output_layer_norm(y_x, x, gt_x, w_out, lg, lb, alpha)
        else:
            p = i // 2
            w_in = pool_w_in[p].astype(bf16)
            grp_w = pool_grp_w[p].astype(bf16)
            w_out = pool_w_out[p].astype(bf16)
            cs = pool_scale[p][None, :]
            ug_x = _project_f32(x, sh_x, sc_x, w_in)
            if ctx_out:
                ug_c = _project_f32(ctx, sh_c, sc_c, w_in)
                ctx = _pool_mix_output(ug_c, ctx, gt_c, grp_w, cs, w_out, lg, lb, alpha)
            x = _pool_mix_output(ug_x, x, gt_x, grp_w, cs, w_out, lg, lb, alpha)
    return x
```

```python
import functools
import math

import jax
import jax.numpy as jnp
from jax import lax
from jax.experimental import pallas as pl
from jax.experimental.pallas import tpu as pltpu

N_HEADS = 8
HEAD_DIM = 128
V_DIM = 2 * HEAD_DIM
GRID_W = 64
ROPE_THETA = 10000.0
ROPE_PAIRS = HEAD_DIM // 4
POOL_WINDOWS = (2, 4, 8, 16)
POOL_HALO = 8
LN_EPS = 1e-5
SUBLN_EPS = 1e-5

LANES = 128
SUBLANES = 8
MXU_DEPTH = 256
VMEM_LIMIT_BYTES = 56 * 1024 * 1024

PROJ_ROWS = 512
PROJ_COLS = 2048
ATTN_Q_ROWS = 1024
ATTN_K_ROWS = 1280
MOD_COLS = 1024
MOD_ROWS = SUBLANES

LOG2E = 1.4426950408889634


def _silu(v):
    return v / (1.0 + jnp.exp(-v))


def _params(*semantics):
    return pltpu.CompilerParams(dimension_semantics=semantics, vmem_limit_bytes=VMEM_LIMIT_BYTES)


def _mod_kernel(c_ref, w_ref, b_ref, o_ref):
    s = _silu(c_ref[...])
    o_ref[...] = jnp.dot(s, w_ref[...], precision=lax.Precision.HIGHEST,
                         preferred_element_type=jnp.float32) + b_ref[...]


def _modulation(cvec, mod_w, mod_b):
    depth, d, n = mod_w.shape
    tn = min(MOD_COLS, n)
    return pl.pallas_call(
        _mod_kernel,
        grid=(depth, n // tn),
        in_specs=[
            pl.BlockSpec((MOD_ROWS, d), lambda l, j: (0, 0)),
            pl.BlockSpec((None, d, tn), lambda l, j: (l, 0, j)),
            pl.BlockSpec((None, 1, tn), lambda l, j: (l, 0, j)),
        ],
        out_specs=pl.BlockSpec((None, MOD_ROWS, tn), lambda l, j: (l, 0, j)),
        out_shape=jax.ShapeDtypeStruct((depth, MOD_ROWS, n), jnp.float32),
        compiler_params=_params("parallel", "parallel"),
        name="modulation",
    )(cvec, mod_w, mod_b.reshape(depth, 1, n))


def _modulated_matmul(x_ref, sh_ref, sc_ref, w_ref):
    h = (x_ref[...] * (1.0 + sc_ref[...]) + sh_ref[...]).astype(jnp.bfloat16)
    return jnp.dot(h, w_ref[...], preferred_element_type=jnp.float32)


def _proj_f32_kernel(x_ref, sh_ref, sc_ref, w_ref, o_ref):
    o_ref[...] = _modulated_matmul(x_ref, sh_ref, sc_ref, w_ref)


def _proj_qk_kernel(x_ref, sh_ref, sc_ref, w_ref, c_ref, sa_ref, sb_ref, o_ref):
    acc = _modulated_matmul(x_ref, sh_ref, sc_ref, w_ref)
    c, sa, sb = c_ref[...], sa_ref[...], sb_ref[...]
    for gi in range(acc.shape[1] // LANES):
        t = acc[:, gi * LANES:(gi + 1) * LANES]
        r = t * c + pltpu.roll(t, LANES - ROPE_PAIRS, 1) * sa + pltpu.roll(t, ROPE_PAIRS, 1) * sb
        head, comp = divmod(gi, V_DIM // LANES)
        o_ref[head, :, comp * LANES:(comp + 1) * LANES] = r.astype(o_ref.dtype)


def _project_qk(x, shift, scale, w_qk, tables):
    b, s, d = x.shape
    tn = PROJ_COLS
    nt = w_qk.shape[1] // tn
    nh = tn // V_DIM
    tm = min(PROJ_ROWS, s)
    tab_spec = pl.BlockSpec((None, tm, LANES), lambda n, bb, i: (n, i, 0))
    return pl.pallas_call(
        _proj_qk_kernel,
        grid=(nt, b, s // tm),
        in_specs=[
            pl.BlockSpec((None, tm, d), lambda n, bb, i: (bb, i, 0)),
            pl.BlockSpec((None, 1, d), lambda n, bb, i: (bb, 0, 0)),
            pl.BlockSpec((None, 1, d), lambda n, bb, i: (bb, 0, 0)),
            pl.BlockSpec((d, tn), lambda n, bb, i: (0, n)),
            tab_spec, tab_spec, tab_spec,
        ],
        out_specs=pl.BlockSpec((None, None, nh, tm, V_DIM), lambda n, bb, i: (n, bb, 0, i, 0)),
        out_shape=jax.ShapeDtypeStruct((nt, b, nh, s, V_DIM), jnp.bfloat16),
        compiler_params=_params("arbitrary", "arbitrary", "arbitrary"),
        name="project_qk",
    )(x, shift, scale, w_qk, *tables)


def _proj_vt_kernel(x_ref, sh_ref, sc_ref, wt_ref, o_ref):
    h = (x_ref[...] * (1.0 + sc_ref[...]) + sh_ref[...]).astype(jnp.bfloat16)
    vt = lax.dot_general(wt_ref[...], h, (((1,), (1,)), ((), ())),
                         preferred_element_type=jnp.float32)
    for head in range(o_ref.shape[0]):
        o_ref[head] = vt[head * V_DIM:(head + 1) * V_DIM, :].astype(o_ref.dtype)


def _project_vt(x, shift, scale, w_vt):
    b, s, d = x.shape
    nh = w_vt.shape[0] // V_DIM
    tm = min(PROJ_ROWS, s)
    return pl.pallas_call(
        _proj_vt_kernel,
        grid=(b, s // tm),
        in_specs=[
            pl.BlockSpec((None, tm, d), lambda bb, i: (bb, i, 0)),
            pl.BlockSpec((None, 1, d), lambda bb, i: (bb, 0, 0)),
            pl.BlockSpec((None, 1, d), lambda bb, i: (bb, 0, 0)),
            pl.BlockSpec(w_vt.shape, lambda bb, i: (0, 0)),
        ],
        out_specs=pl.BlockSpec((None, nh, V_DIM, tm), lambda bb, i: (bb, 0, 0, i)),
        out_shape=jax.ShapeDtypeStruct((b, nh, V_DIM, s), jnp.bfloat16),
        compiler_params=_params("parallel", "parallel"),
        name="project_vt",
    )(x, shift, scale, w_vt)


def _project_f32(x, shift, scale, w):
    b, s, d = x.shape
    tn = PROJ_COLS
    nt = w.shape[1] // tn
    tm = min(PROJ_ROWS, s)
    return pl.pallas_call(
        _proj_f32_kernel,
        grid=(nt, b, s // tm),
        in_specs=[
            pl.BlockSpec((None, tm, d), lambda n, bb, i: (bb, i, 0)),
            pl.BlockSpec((None, 1, d), lambda n, bb, i: (bb, 0, 0)),
            pl.BlockSpec((None, 1, d), lambda n, bb, i: (bb, 0, 0)),
            pl.BlockSpec((d, tn), lambda n, bb, i: (0, n)),
        ],
        out_specs=pl.BlockSpec((None, tm, tn), lambda n, bb, i: (bb, i, n)),
        out_shape=jax.ShapeDtypeStruct((b, s, nt * tn), jnp.float32),
        compiler_params=_params("arbitrary", "arbitrary", "arbitrary"),
        name="project_f32",
    )(x, shift, scale, w)


def _key_block(skv):
    best = MXU_DEPTH
    for cand in range(MXU_DEPTH, min(ATTN_K_ROWS, skv) + 1, MXU_DEPTH):
        if skv % cand == 0:
            best = cand
    return best


def _interleave(major, minor):
    out = []
    placed = 0
    for idx, item in enumerate(major):
        while placed < len(minor) and placed * len(major) <= idx * len(minor):
            out.append(minor[placed])
            placed += 1
        out.append(item)
    return out + list(minor[placed:])


def _attn_kernel(lq1_ref, lk1_ref, lq2_ref, lk2_ref, subg_ref, q_ref, k0_ref, kn_ref, vt_ref, g_ref, o_ref,
                 s_sc, mx_sc, m_sc, l_sc, acc_sc, *, lam_init):
    j = pl.program_id(3)
    tk = kn_ref.shape[0]
    tq = q_ref.shape[0]
    n_key_chunks = tk // MXU_DEPTH
    n_query_chunks = tq // MXU_DEPTH

    def scores_piece(kb_ref, slot, comp, qc):
        cols = slice(comp * HEAD_DIM, (comp + 1) * HEAD_DIM)
        qs = slice(qc * MXU_DEPTH, (qc + 1) * MXU_DEPTH)
        st = lax.dot_general(kb_ref[:, cols], q_ref[qs, cols], (((1,), (1,)), ((), ())),
                             preferred_element_type=jnp.float32)
        s_sc[slot, comp, :, qs] = st
        mx_sc[slot, comp, :, qs] = jnp.max(st.reshape(tk // SUBLANES, SUBLANES, MXU_DEPTH), axis=0)

    def block_stats(slot, comp):
        m_blk = jnp.max(mx_sc[slot, comp], axis=0, keepdims=True)
        m_prev = m_sc[comp]
        m_new = jnp.maximum(m_prev, m_blk)
        alpha = jnp.exp2(m_prev - m_new)
        m_sc[comp] = m_new
        return jnp.broadcast_to(m_new, (SUBLANES, tq)), alpha

    def step(slot):
        stats = [block_stats(slot, comp) for comp in range(2)]
        sums = [None, None]
        outs = [None, None]
        pieces = _interleave([("pv", comp, kc) for comp in range(2) for kc in range(n_key_chunks)],
                             [("qk", comp, qc) for comp in range(2) for qc in range(n_query_chunks)])
        for kind, comp, idx in pieces:
            if kind == "qk":
                scores_piece(kn_ref, 1 - slot, comp, idx)
                continue
            m_b, alpha = stats[comp]
            rows = slice(idx * MXU_DEPTH, (idx + 1) * MXU_DEPTH)
            st = s_sc[slot, comp, rows, :].reshape(MXU_DEPTH // SUBLANES, SUBLANES, tq)
            p = jnp.exp2(st - m_b)
            part = jnp.sum(p, axis=0)
            sums[comp] = part if sums[comp] is None else sums[comp] + part
            pv = jnp.dot(vt_ref[:, rows], p.reshape(MXU_DEPTH, tq).astype(jnp.bfloat16),
                         preferred_element_type=jnp.float32)
            outs[comp] = pv if outs[comp] is None else outs[comp] + pv
            if idx == n_key_chunks - 1:
                l_sc[comp] = alpha * l_sc[comp] + sums[comp]
                acc_sc[comp] = alpha * acc_sc[comp] + outs[comp]

    @pl.when(j == 0)
    def _():
        m_sc[...] = jnp.full(m_sc.shape, -jnp.inf, jnp.float32)
        l_sc[...] = jnp.zeros(l_sc.shape, jnp.float32)
        acc_sc[...] = jnp.zeros(acc_sc.shape, jnp.float32)
        for comp in range(2):
            for qc in range(n_query_chunks):
                scores_piece(k0_ref, 0, comp, qc)

    for parity in range(2):
        @pl.when(j % 2 == parity)
        def _():
            step(parity)

    @pl.when(j == pl.num_programs(3) - 1)
    def _():
        lam = (jnp.exp(jnp.sum(lq1_ref[...] * lk1_ref[...], axis=-1, keepdims=True))
               - jnp.exp(jnp.sum(lq2_ref[...] * lk2_ref[...], axis=-1, keepdims=True)) + lam_init)
        l1 = jnp.sum(l_sc[0], axis=0, keepdims=True)
        l2 = jnp.sum(l_sc[1], axis=0, keepdims=True)
        ot = acc_sc[0] / l1 - lam * (acc_sc[1] / l2)
        ot = ot * lax.rsqrt(jnp.mean(ot * ot, axis=0, keepdims=True) + SUBLN_EPS)
        o = ot.T * subg_ref[...] * (1.0 - lam_init)
        o_ref[...] = (o * _silu(g_ref[...])).astype(o_ref.dtype)


def _attention(qk, k, vt, gate_path, lams, subln_g, lam_init):
    _, b, nh, sq, _ = qk.shape
    skv = k.shape[2]
    tq = min(ATTN_Q_ROWS, sq)
    tk = _key_block(skv)
    nk = skv // tk
    small = pl.BlockSpec((1, HEAD_DIM), lambda bb, h, i, j: (0, 0))
    return pl.pallas_call(
        functools.partial(_attn_kernel, lam_init=lam_init),
        grid=(b, nh, sq // tq, nk),
        in_specs=[
            small, small, small, small,
            pl.BlockSpec((1, V_DIM), lambda bb, h, i, j: (0, 0)),
            pl.BlockSpec((None, None, None, tq, V_DIM), lambda bb, h, i, j: (0, bb, h, i, 0)),
            pl.BlockSpec((None, None, tk, V_DIM), lambda bb, h, i, j: (bb, h, 0, 0)),
            pl.BlockSpec((None, None, tk, V_DIM), lambda bb, h, i, j: (bb, h, jnp.minimum(j + 1, nk - 1), 0)),
            pl.BlockSpec((None, None, V_DIM, tk), lambda bb, h, i, j: (bb, h, 0, j)),
            pl.BlockSpec((None, tq, V_DIM), lambda bb, h, i, j: (bb, i, h)),
        ],
        out_specs=pl.BlockSpec((None, tq, V_DIM), lambda bb, h, i, j: (bb, i, h)),
        out_shape=jax.ShapeDtypeStruct((b, sq, nh * V_DIM), jnp.bfloat16),
        scratch_shapes=[
            pltpu.VMEM((2, 2, tk, tq), jnp.float32),
            pltpu.VMEM((2, 2, SUBLANES, tq), jnp.float32),
            pltpu.VMEM((2, 1, tq), jnp.float32),
            pltpu.VMEM((2, SUBLANES, tq), jnp.float32),
            pltpu.VMEM((2, V_DIM, tq), jnp.float32),
        ],
        compiler_params=_params("parallel", "parallel", "parallel", "arbitrary"),
        name="diff_attention",
    )(*lams, subln_g, qk, k, k, vt, gate_path)


def _residual_layer_norm(x, gate, y, ln_g, ln_b, alpha):
    z = alpha * x + gate * y
    mu = jnp.mean(z, axis=-1, keepdims=True)
    zc = z - mu
    var = jnp.mean(zc * zc, axis=-1, keepdims=True)
    return zc * lax.rsqrt(var + LN_EPS) * ln_g + ln_b


def _out_kernel(y_ref, x_ref, gate_ref, w_ref, lg_ref, lb_ref, o_ref, *, alpha):
    y = jnp.dot(y_ref[...], w_ref[...], preferred_element_type=jnp.float32)
    o_ref[...] = _residual_layer_norm(x_ref[...], gate_ref[...], y, lg_ref[...], lb_ref[...], alpha)


def _output_layer_norm(y, x, gate, w_out, ln_g, ln_b, alpha):
    b, s, d = x.shape
    tm = min(PROJ_ROWS, s)
    vec = pl.BlockSpec((1, d), lambda bb, i: (0, 0))
    return pl.pallas_call(
        functools.partial(_out_kernel, alpha=alpha),
        grid=(b, s // tm),
        in_specs=[
            pl.BlockSpec((None, tm, y.shape[2]), lambda bb, i: (bb, i, 0)),
            pl.BlockSpec((None, tm, d), lambda bb, i: (bb, i, 0)),
            pl.BlockSpec((None, 1, d), lambda bb, i: (bb, 0, 0)),
            pl.BlockSpec(w_out.shape, lambda bb, i: (0, 0)),
            vec, vec,
        ],
        out_specs=pl.BlockSpec((None, tm, d), lambda bb, i: (bb, i, 0)),
        out_shape=jax.ShapeDtypeStruct((b, s, d), jnp.float32),
        compiler_params=_params("parallel", "parallel"),
        name="output_layer_norm",
    )(y, x, gate, w_out, ln_g, ln_b)


def _pool_kernel(u_ref, up_ref, un_ref, g_ref, x_ref, gate_ref, gw_ref, cs_ref, w_ref, lg_ref, lb_ref,
                 o_ref, ubuf, mix, *, seq, alpha):
    i = pl.program_id(1)
    tm = u_ref.shape[0]
    gdim = u_ref.shape[1] // len(POOL_WINDOWS)
    ubuf[0:POOL_HALO, :] = jnp.where(i > 0, up_ref[...], 0.0)
    ubuf[POOL_HALO:POOL_HALO + tm, :] = u_ref[...]
    ubuf[POOL_HALO + tm:2 * POOL_HALO + tm, :] = jnp.where(i < pl.num_programs(1) - 1, un_ref[...], 0.0)
    t = i * tm + lax.broadcasted_iota(jnp.int32, (tm, 1), 0)
    for gi, w in enumerate(POOL_WINDOWS):
        lo = w // 2
        hi = w - 1 - lo
        cols = slice(gi * gdim, (gi + 1) * gdim)
        win = ubuf[POOL_HALO - lo:POOL_HALO - lo + tm, cols]
        for off in range(-lo + 1, hi + 1):
            win = win + ubuf[POOL_HALO + off:POOL_HALO + off + tm, cols]
        cnt = (jnp.minimum(t + hi + 1, seq) - jnp.maximum(t - lo, 0)).astype(jnp.float32)
        pooled = win / cnt - u_ref[:, cols]
        mix[:, cols] = jnp.dot(pooled.astype(jnp.bfloat16), gw_ref[gi], preferred_element_type=jnp.float32)
    gated = ((mix[...] * cs_ref[...]) * _silu(g_ref[...])).astype(jnp.bfloat16)
    y = jnp.dot(gated, w_ref[...], preferred_element_type=jnp.float32)
    o_ref[...] = _residual_layer_norm(x_ref[...], gate_ref[...], y, lg_ref[...], lb_ref[...], alpha)


def _pool_mix_output(ug, x, gate, grp_w, ch_scale, w_out, ln_g, ln_b, alpha):
    b, s, d = x.shape
    width = ug.shape[2] // 2
    tm = min(PROJ_ROWS, s)
    halo_blocks = tm // POOL_HALO
    last_halo = s // POOL_HALO - 1
    vec = pl.BlockSpec((1, d), lambda bb, i: (0, 0))
    return pl.pallas_call(
        functools.partial(_pool_kernel, seq=s, alpha=alpha),
        grid=(b, s // tm),
        in_specs=[
            pl.BlockSpec((None, tm, width), lambda bb, i: (bb, i, 0)),
            pl.BlockSpec((None, POOL_HALO, width), lambda bb, i: (bb, jnp.maximum(i * halo_blocks - 1, 0), 0)),
            pl.BlockSpec((None, POOL_HALO, width),
                         lambda bb, i: (bb, jnp.minimum((i + 1) * halo_blocks, last_halo), 0)),
            pl.BlockSpec((None, tm, width), lambda bb, i: (bb, i, 1)),
            pl.BlockSpec((None, tm, d), lambda bb, i: (bb, i, 0)),
            pl.BlockSpec((None, 1, d), lambda bb, i: (bb, 0, 0)),
            pl.BlockSpec(grp_w.shape, lambda bb, i: (0, 0, 0)),
            pl.BlockSpec((1, width), lambda bb, i: (0, 0)),
            pl.BlockSpec(w_out.shape, lambda bb, i: (0, 0)),
            vec, vec,
        ],
        out_specs=pl.BlockSpec((None, tm, d), lambda bb, i: (bb, i, 0)),
        out_shape=jax.ShapeDtypeStruct((b, s, d), jnp.float32),
        scratch_shapes=[
            pltpu.VMEM((tm + 2 * POOL_HALO, width), jnp.float32),
            pltpu.VMEM((tm, width), jnp.float32),
        ],
        compiler_params=_params("parallel", "parallel"),
        name="pool_mix_output",
    )(ug, ug, ug, ug, x, gate, grp_w, ch_scale, w_out, ln_g, ln_b)


def _rope_tables(s, q_scale):
    t = jnp.arange(s)
    t_row = (t // GRID_W).astype(jnp.float32)
    t_col = (t % GRID_W).astype(jnp.float32)
    inv_freq = ROPE_THETA ** (-jnp.arange(ROPE_PAIRS, dtype=jnp.float32) / ROPE_PAIRS)
    ang_r = t_row[:, None] * inv_freq
    ang_c = t_col[:, None] * inv_freq
    cr, sr, cc, sc = jnp.cos(ang_r), jnp.sin(ang_r), jnp.cos(ang_c), jnp.sin(ang_c)
    z = jnp.zeros_like(sr)
    c = jnp.concatenate([cr, cr, cc, cc], axis=-1)
    sa = jnp.concatenate([-sr, z, -sc, z], axis=-1)
    sb = jnp.concatenate([z, sr, z, sc], axis=-1)
    return tuple(jnp.stack([tab * q_scale, tab]) for tab in (c, sa, sb))


def _identity_tables(s, q_scale):
    one = jnp.ones((s, LANES), jnp.float32)
    zero = jnp.zeros((2, s, LANES), jnp.float32)
    return jnp.stack([one * q_scale, one]), zero, zero


def kernel(x, c, ctx, c_ctx, mod_w, mod_b, ln_g, ln_b, attn_w_in, attn_w_out, attn_lq1, attn_lk1, attn_lq2,
           attn_lk2, attn_subln_g, pool_w_in, pool_grp_w, pool_scale, pool_w_out):
    b, s, d = x.shape
    ctx_len = ctx.shape[1]
    depth = mod_w.shape[0]
    assert d == N_HEADS * V_DIM and b + 1 <= MOD_ROWS
    assert s % PROJ_ROWS == 0 or s < PROJ_ROWS
    alpha = (2 * depth) ** 0.25
    q_scale = LOG2E / math.sqrt(HEAD_DIM)
    bf16 = jnp.bfloat16

    cvec = jnp.concatenate([c, c_ctx[None, :], jnp.zeros((MOD_ROWS - b - 1, d), jnp.float32)], axis=0)
    mod = _modulation(cvec, mod_w, mod_b)
    tabs_x = _rope_tables(s, q_scale)
    tabs_c = _identity_tables(ctx_len, q_scale)

    for i in range(depth):
        is_attn = i % 2 == 0
        ctx_out = any(j % 2 == 0 for j in range(i + 1, depth))
        shift, scale, gate = (mod[i, :, k * d:(k + 1) * d] for k in range(3))
        sh_x, sc_x, gt_x = (v[:b, None, :] for v in (shift, scale, gate))
        sh_c, sc_c, gt_c = (jnp.broadcast_to(v[b][None, None, :], (b, 1, d)) for v in (shift, scale, gate))
        lg, lb = ln_g[i][None, :], ln_b[i][None, :]
        if is_attn:
            a = i // 2
            lam_init = 0.8 - 0.6 * math.exp(-0.3 * i)
            w_in = attn_w_in[a].astype(bf16)
            w_qk, w_vt, w_gate = w_in[:, :2 * d], w_in[:, 2 * d:3 * d].T, w_in[:, 3 * d:]
            w_out = attn_w_out[a].astype(bf16)
            lams = [v[a][None, :] for v in (attn_lq1, attn_lk1, attn_lq2, attn_lk2)]
            subg = attn_subln_g[a][None, :]
            qk_x = _project_qk(x, sh_x, sc_x, w_qk, tabs_x)
            qk_c = _project_qk(ctx, sh_c, sc_c, w_qk, tabs_c)
            vt_x = _project_vt(x, sh_x, sc_x, w_vt)
            vt_c = _project_vt(ctx, sh_c, sc_c, w_vt)
            g_x = _project_f32(x, sh_x, sc_x, w_gate)
            k_all = jnp.concatenate([qk_c[1], qk_x[1]], axis=2)
            vt_all = jnp.concatenate([vt_c, vt_x], axis=3)
            y_x = _attention(qk_x, k_all, vt_all, g_x, lams, subg, lam_init)
            if ctx_out:
                g_c = _project_f32(ctx, sh_c, sc_c, w_gate)
                y_c = _attention(qk_c, qk_c[1], vt_c, g_c, lams, subg, lam_init)
                ctx = _output_layer_norm(y_c, ctx, gt_c, w_out, lg, lb, alpha)
            x = _output_layer_norm(y_x, x, gt_x, w_out, lg, lb, alpha)
        else:
            p = i // 2
            w_in = pool_w_in[p].astype(bf16)
            grp_w = pool_grp_w[p].astype(bf16)
            w_out = pool_w_out[p].astype(bf16)
            cs = pool_scale[p][None, :]
            ug_x = _project_f32(x, sh_x, sc_x, w_in)
            if ctx_out:
                ug_c = _project_f32(ctx, sh_c, sc_c, w_in)
                ctx = _pool_mix_output(ug_c, ctx, gt_c, grp_w, cs, w_out, lg, lb, alpha)
            x = _pool_mix_output(ug_x, x, gt_x, grp_w, cs, w_out, lg, lb, alpha)
    return x
```

```python
import functools
import math

import jax
import jax.numpy as jnp
from jax import lax
from jax.experimental import pallas as pl
from jax.experimental.pallas import tpu as pltpu

N_HEADS = 8
HEAD_DIM = 128
V_DIM = 2 * HEAD_DIM
GRID_W = 64
ROPE_THETA = 10000.0
ROPE_PAIRS = HEAD_DIM // 4
POOL_WINDOWS = (2, 4, 8, 16)
POOL_HALO = 8
LN_EPS = 1e-5
SUBLN_EPS = 1e-5

LANES = 128
SUBLANES = 8
MXU_DEPTH = 256
VMEM_LIMIT_BYTES = 56 * 1024 * 1024

PROJ_ROWS = 512
PROJ_COLS = 2048
ATTN_Q_ROWS = 1024
ATTN_K_ROWS = 1280
MOD_COLS = 1024
MOD_ROWS = SUBLANES

LOG2E = 1.4426950408889634


def _silu(v):
    return v / (1.0 + jnp.exp(-v))


def _params(*semantics):
    return pltpu.CompilerParams(dimension_semantics=semantics, vmem_limit_bytes=VMEM_LIMIT_BYTES)


def _mod_kernel(c_ref, w_ref, b_ref, o_ref):
    s = _silu(c_ref[...])
    o_ref[...] = jnp.dot(s, w_ref[...], precision=lax.Precision.HIGHEST,
                         preferred_element_type=jnp.float32) + b_ref[...]


def _modulation(cvec, mod_w, mod_b):
    depth, d, n = mod_w.shape
    tn = min(MOD_COLS, n)
    return pl.pallas_call(
        _mod_kernel,
        grid=(depth, n // tn),
        in_specs=[
            pl.BlockSpec((MOD_ROWS, d), lambda l, j: (0, 0)),
            pl.BlockSpec((None, d, tn), lambda l, j: (l, 0, j)),
            pl.BlockSpec((None, 1, tn), lambda l, j: (l, 0, j)),
        ],
        out_specs=pl.BlockSpec((None, MOD_ROWS, tn), lambda l, j: (l, 0, j)),
        out_shape=jax.ShapeDtypeStruct((depth, MOD_ROWS, n), jnp.float32),
        compiler_params=_params("parallel", "parallel"),
        name="modulation",
    )(cvec, mod_w, mod_b.reshape(depth, 1, n))


def _modulated_matmul(x_ref, sh_ref, sc_ref, w_ref):
    h = (x_ref[...] * (1.0 + sc_ref[...]) + sh_ref[...]).astype(jnp.bfloat16)
    return jnp.dot(h, w_ref[...], preferred_element_type=jnp.float32)


def _proj_f32_kernel(x_ref, sh_ref, sc_ref, w_ref, o_ref):
    o_ref[...] = _modulated_matmul(x_ref, sh_ref, sc_ref, w_ref)


def _proj_qk_kernel(x_ref, sh_ref, sc_ref, w_ref, c_ref, sa_ref, sb_ref, *rest):
    o_ref = rest[-1]
    acc = _modulated_matmul(x_ref, sh_ref, sc_ref, w_ref)
    c, sa, sb = c_ref[...], sa_ref[...], sb_ref[...]
    for gi in range(acc.shape[1] // LANES):
        t = acc[:, gi * LANES:(gi + 1) * LANES]
        r = t * c + pltpu.roll(t, LANES - ROPE_PAIRS, 1) * sa + pltpu.roll(t, ROPE_PAIRS, 1) * sb
        head, comp = divmod(gi, V_DIM // LANES)
        o_ref[head, :, comp * LANES:(comp + 1) * LANES] = r.astype(o_ref.dtype)


def _project_qk(x, shift, scale, w_qk, tables, total_rows, row_start, into=None):
    b, s, d = x.shape
    tn = PROJ_COLS
    nt = w_qk.shape[1] // tn
    nh = tn // V_DIM
    tm = min(PROJ_ROWS, s)
    assert row_start % tm == 0
    first = row_start // tm
    tab_spec = pl.BlockSpec((None, tm, LANES), lambda n, bb, i: (n, i, 0))
    in_specs = [
        pl.BlockSpec((None, tm, d), lambda n, bb, i: (bb, i, 0)),
        pl.BlockSpec((None, 1, d), lambda n, bb, i: (bb, 0, 0)),
        pl.BlockSpec((None, 1, d), lambda n, bb, i: (bb, 0, 0)),
        pl.BlockSpec((d, tn), lambda n, bb, i: (0, n)),
        tab_spec, tab_spec, tab_spec,
    ]
    args = [x, shift, scale, w_qk, *tables]
    aliases = {}
    if into is not None:
        aliases = {len(args): 0}
        in_specs.append(pl.BlockSpec(memory_space=pl.ANY))
        args.append(into)
    return pl.pallas_call(
        _proj_qk_kernel,
        grid=(nt, b, s // tm),
        in_specs=in_specs,
        out_specs=pl.BlockSpec((None, None, nh, tm, V_DIM), lambda n, bb, i: (n, bb, 0, first + i, 0)),
        out_shape=jax.ShapeDtypeStruct((nt, b, nh, total_rows, V_DIM), jnp.bfloat16),
        input_output_aliases=aliases,
        compiler_params=_params("arbitrary", "arbitrary", "arbitrary"),
        name="project_qk",
    )(*args)


def _proj_vt_kernel(x_ref, sh_ref, sc_ref, wt_ref, *rest):
    o_ref = rest[-1]
    h = (x_ref[...] * (1.0 + sc_ref[...]) + sh_ref[...]).astype(jnp.bfloat16)
    vt = lax.dot_general(wt_ref[...], h, (((1,), (1,)), ((), ())),
                         preferred_element_type=jnp.float32)
    for head in range(o_ref.shape[0]):
        o_ref[head] = vt[head * V_DIM:(head + 1) * V_DIM, :].astype(o_ref.dtype)


def _project_vt(x, shift, scale, w_vt, total_rows, row_start, into=None):
    b, s, d = x.shape
    nh = w_vt.shape[0] // V_DIM
    tm = min(PROJ_ROWS, s)
    assert row_start % tm == 0
    first = row_start // tm
    in_specs = [
        pl.BlockSpec((None, tm, d), lambda bb, i: (bb, i, 0)),
        pl.BlockSpec((None, 1, d), lambda bb, i: (bb, 0, 0)),
        pl.BlockSpec((None, 1, d), lambda bb, i: (bb, 0, 0)),
        pl.BlockSpec(w_vt.shape, lambda bb, i: (0, 0)),
    ]
    args = [x, shift, scale, w_vt]
    aliases = {}
    if into is not None:
        aliases = {len(args): 0}
        in_specs.append(pl.BlockSpec(memory_space=pl.ANY))
        args.append(into)
    return pl.pallas_call(
        _proj_vt_kernel,
        grid=(b, s // tm),
        in_specs=in_specs,
        out_specs=pl.BlockSpec((None, nh, V_DIM, tm), lambda bb, i: (bb, 0, 0, first + i)),
        out_shape=jax.ShapeDtypeStruct((b, nh, V_DIM, total_rows), jnp.bfloat16),
        input_output_aliases=aliases,
        compiler_params=_params("parallel", "parallel"),
        name="project_vt",
    )(*args)


def _project_f32(x, shift, scale, w):
    b, s, d = x.shape
    tn = PROJ_COLS
    nt = w.shape[1] // tn
    tm = min(PROJ_ROWS, s)
    return pl.pallas_call(
        _proj_f32_kernel,
        grid=(nt, b, s // tm),
        in_specs=[
            pl.BlockSpec((None, tm, d), lambda n, bb, i: (bb, i, 0)),
            pl.BlockSpec((None, 1, d), lambda n, bb, i: (bb, 0, 0)),
            pl.BlockSpec((None, 1, d), lambda n, bb, i: (bb, 0, 0)),
            pl.BlockSpec((d, tn), lambda n, bb, i: (0, n)),
        ],
        out_specs=pl.BlockSpec((None, tm, tn), lambda n, bb, i: (bb, i, n)),
        out_shape=jax.ShapeDtypeStruct((b, s, nt * tn), jnp.float32),
        compiler_params=_params("arbitrary", "arbitrary", "arbitrary"),
        name="project_f32",
    )(x, shift, scale, w)


def _key_block(skv):
    best = MXU_DEPTH
    for cand in range(MXU_DEPTH, min(ATTN_K_ROWS, skv) + 1, MXU_DEPTH):
        if skv % cand == 0:
            best = cand
    return best


def _interleave(major, minor):
    out = []
    placed = 0
    for idx, item in enumerate(major):
        while placed < len(minor) and placed * len(major) <= idx * len(minor):
            out.append(minor[placed])
            placed += 1
        out.append(item)
    return out + list(minor[placed:])


def _attn_kernel(lq1_ref, lk1_ref, lq2_ref, lk2_ref, subg_ref, q_ref, k0_ref, kn_ref, vt_ref, g_ref, o_ref,
                 s_sc, mx_sc, m_sc, l_sc, acc_sc, *, lam_init):
    i = pl.program_id(2)
    j = pl.program_id(3)
    tk = kn_ref.shape[0]
    tq = q_ref.shape[0]
    n_key_chunks = tk // MXU_DEPTH
    n_query_chunks = tq // MXU_DEPTH

    def scores_piece(kb_ref, slot, comp, qc):
        cols = slice(comp * HEAD_DIM, (comp + 1) * HEAD_DIM)
        qs = slice(qc * MXU_DEPTH, (qc + 1) * MXU_DEPTH)
        st = lax.dot_general(kb_ref[:, cols], q_ref[qs, cols], (((1,), (1,)), ((), ())),
                             preferred_element_type=jnp.float32)
        s_sc[slot, comp, :, qs] = st
        mx_sc[slot, comp, :, qs] = jnp.max(st.reshape(tk // SUBLANES, SUBLANES, MXU_DEPTH), axis=0)

    def block_stats(slot, comp):
        m_blk = jnp.max(mx_sc[slot, comp], axis=0, keepdims=True)
        m_prev = m_sc[comp]
        m_new = jnp.maximum(m_prev, m_blk)
        alpha = jnp.exp2(m_prev - m_new)
        m_sc[comp] = m_new
        return jnp.broadcast_to(m_new, (SUBLANES, tq)), alpha

    def step(slot):
        stats = [block_stats(slot, comp) for comp in range(2)]
        sums = [None, None]
        outs = [None, None]
        pieces = _interleave([("pv", comp, kc) for comp in range(2) for kc in range(n_key_chunks)],
                             [("qk", comp, qc) for comp in range(2) for qc in range(n_query_chunks)])
        for kind, comp, idx in pieces:
            if kind == "qk":
                scores_piece(kn_ref, 1 - slot, comp, idx)
                continue
            m_b, alpha = stats[comp]
            rows = slice(idx * MXU_DEPTH, (idx + 1) * MXU_DEPTH)
            st = s_sc[slot, comp, rows, :].reshape(MXU_DEPTH // SUBLANES, SUBLANES, tq)
            p = jnp.exp2(st - m_b)
            part = jnp.sum(p, axis=0)
            sums[comp] = part if sums[comp] is None else sums[comp] + part
            pv = jnp.dot(vt_ref[:, rows], p.reshape(MXU_DEPTH, tq).astype(jnp.bfloat16),
                         preferred_element_type=jnp.float32)
            outs[comp] = pv if outs[comp] is None else outs[comp] + pv
            if idx == n_key_chunks - 1:
                l_sc[comp] = alpha * l_sc[comp] + sums[comp]
                acc_sc[comp] = alpha * acc_sc[comp] + outs[comp]

    @pl.when((i == 0) & (j == 0))
    def _():
        for comp in range(2):
            for qc in range(n_query_chunks):
                scores_piece(k0_ref, 0, comp, qc)

    @pl.when(j == 0)
    def _():
        m_sc[...] = jnp.full(m_sc.shape, -jnp.inf, jnp.float32)
        l_sc[...] = jnp.zeros(l_sc.shape, jnp.float32)
        acc_sc[...] = jnp.zeros(acc_sc.shape, jnp.float32)

    for parity in range(2):
        @pl.when((i * pl.num_programs(3) + j) % 2 == parity)
        def _():
            step(parity)

    @pl.when(j == pl.num_programs(3) - 1)
    def _():
        lam = (jnp.exp(jnp.sum(lq1_ref[...] * lk1_ref[...], axis=-1, keepdims=True))
               - jnp.exp(jnp.sum(lq2_ref[...] * lk2_ref[...], axis=-1, keepdims=True)) + lam_init)
        l1 = jnp.sum(l_sc[0], axis=0, keepdims=True)
        l2 = jnp.sum(l_sc[1], axis=0, keepdims=True)
        ot = acc_sc[0] / l1 - lam * (acc_sc[1] / l2)
        ot = ot * lax.rsqrt(jnp.mean(ot * ot, axis=0, keepdims=True) + SUBLN_EPS)
        o = ot.T * subg_ref[...] * (1.0 - lam_init)
        o_ref[...] = (o * _silu(g_ref[...])).astype(o_ref.dtype)


def _attention(qk, vt, gate_path, lams, subln_g, lam_init, q_rows, kv_rows):
    _, b, nh, _, _ = qk.shape
    (q_start, sq), (kv_start, skv) = q_rows, kv_rows
    tq = min(ATTN_Q_ROWS, sq)
    tk = _key_block(skv)
    assert sq % tq == 0 and q_start % tq == 0 and kv_start % tk == 0
    nq, nk = sq // tq, skv // tk
    q0, k0 = q_start // tq, kv_start // tk
    small = pl.BlockSpec((1, HEAD_DIM), lambda bb, h, i, j: (0, 0))
    return pl.pallas_call(
        functools.partial(_attn_kernel, lam_init=lam_init),
        grid=(b, nh, nq, nk),
        in_specs=[
            small, small, small, small,
            pl.BlockSpec((1, V_DIM), lambda bb, h, i, j: (0, 0)),
            pl.BlockSpec((None, None, None, tq, V_DIM),
                         lambda bb, h, i, j: (0, bb, h, q0 + jnp.minimum(i + (j + 1) // nk, nq - 1), 0)),
            pl.BlockSpec((None, None, None, tk, V_DIM), lambda bb, h, i, j: (1, bb, h, k0, 0)),
            pl.BlockSpec((None, None, None, tk, V_DIM), lambda bb, h, i, j: (1, bb, h, k0 + (j + 1) % nk, 0)),
            pl.BlockSpec((None, None, V_DIM, tk), lambda bb, h, i, j: (bb, h, 0, k0 + j)),
            pl.BlockSpec((None, tq, V_DIM), lambda bb, h, i, j: (bb, i, h)),
        ],
        out_specs=pl.BlockSpec((None, tq, V_DIM), lambda bb, h, i, j: (bb, i, h)),
        out_shape=jax.ShapeDtypeStruct((b, sq, nh * V_DIM), jnp.bfloat16),
        scratch_shapes=[
            pltpu.VMEM((2, 2, tk, tq), jnp.float32),
            pltpu.VMEM((2, 2, SUBLANES, tq), jnp.float32),
            pltpu.VMEM((2, 1, tq), jnp.float32),
            pltpu.VMEM((2, SUBLANES, tq), jnp.float32),
            pltpu.VMEM((2, V_DIM, tq), jnp.float32),
        ],
        compiler_params=_params("parallel", "parallel", "arbitrary", "arbitrary"),
        name="diff_attention",
    )(*lams, subln_g, qk, qk, qk, vt, gate_path)


def _residual_layer_norm(x, gate, y, ln_g, ln_b, alpha):
    z = alpha * x + gate * y
    mu = jnp.mean(z, axis=-1, keepdims=True)
    zc = z - mu
    var = jnp.mean(zc * zc, axis=-1, keepdims=True)
    return zc * lax.rsqrt(var + LN_EPS) * ln_g + ln_b


def _out_kernel(y_ref, x_ref, gate_ref, w_ref, lg_ref, lb_ref, o_ref, *, alpha):
    y = jnp.dot(y_ref[...], w_ref[...], preferred_element_type=jnp.float32)
    o_ref[...] = _residual_layer_norm(x_ref[...], gate_ref[...], y, lg_ref[...], lb_ref[...], alpha)


def _output_layer_norm(y, x, gate, w_out, ln_g, ln_b, alpha):
    b, s, d = x.shape
    tm = min(PROJ_ROWS, s)
    vec = pl.BlockSpec((1, d), lambda bb, i: (0, 0))
    return pl.pallas_call(
        functools.partial(_out_kernel, alpha=alpha),
        grid=(b, s // tm),
        in_specs=[
            pl.BlockSpec((None, tm, y.shape[2]), lambda bb, i: (bb, i, 0)),
            pl.BlockSpec((None, tm, d), lambda bb, i: (bb, i, 0)),
            pl.BlockSpec((None, 1, d), lambda bb, i: (bb, 0, 0)),
            pl.BlockSpec(w_out.shape, lambda bb, i: (0, 0)),
            vec, vec,
        ],
        out_specs=pl.BlockSpec((None, tm, d), lambda bb, i: (bb, i, 0)),
        out_shape=jax.ShapeDtypeStruct((b, s, d), jnp.float32),
        compiler_params=_params("parallel", "parallel"),
        name="output_layer_norm",
    )(y, x, gate, w_out, ln_g, ln_b)


def _pool_kernel(u_ref, up_ref, un_ref, g_ref, x_ref, gate_ref, gw_ref, cs_ref, w_ref, lg_ref, lb_ref,
                 o_ref, ubuf, mix, *, seq, alpha):
    i = pl.program_id(1)
    tm = u_ref.shape[0]
    gdim = u_ref.shape[1] // len(POOL_WINDOWS)
    ubuf[0:POOL_HALO, :] = jnp.where(i > 0, up_ref[...], 0.0)
    ubuf[POOL_HALO:POOL_HALO + tm, :] = u_ref[...]
    ubuf[POOL_HALO + tm:2 * POOL_HALO + tm, :] = jnp.where(i < pl.num_programs(1) - 1, un_ref[...], 0.0)
    t = i * tm + lax.broadcasted_iota(jnp.int32, (tm, 1), 0)
    for gi, w in enumerate(POOL_WINDOWS):
        lo = w // 2
        hi = w - 1 - lo
        cols = slice(gi * gdim, (gi + 1) * gdim)
        win = ubuf[POOL_HALO - lo:POOL_HALO - lo + tm, cols]
        for off in range(-lo + 1, hi + 1):
            win = win + ubuf[POOL_HALO + off:POOL_HALO + off + tm, cols]
        cnt = (jnp.minimum(t + hi + 1, seq) - jnp.maximum(t - lo, 0)).astype(jnp.float32)
        pooled = win / cnt - u_ref[:, cols]
        mix[:, cols] = jnp.dot(pooled.astype(jnp.bfloat16), gw_ref[gi], preferred_element_type=jnp.float32)
    gated = ((mix[...] * cs_ref[...]) * _silu(g_ref[...])).astype(jnp.bfloat16)
    y = jnp.dot(gated, w_ref[...], preferred_element_type=jnp.float32)
    o_ref[...] = _residual_layer_norm(x_ref[...], gate_ref[...], y, lg_ref[...], lb_ref[...], alpha)


def _pool_mix_output(ug, x, gate, grp_w, ch_scale, w_out, ln_g, ln_b, alpha):
    b, s, d = x.shape
    width = ug.shape[2] // 2
    tm = min(PROJ_ROWS, s)
    halo_blocks = tm // POOL_HALO
    last_halo = s // POOL_HALO - 1
    vec = pl.BlockSpec((1, d), lambda bb, i: (0, 0))
    return pl.pallas_call(
        functools.partial(_pool_kernel, seq=s, alpha=alpha),
        grid=(b, s // tm),
        in_specs=[
            pl.BlockSpec((None, tm, width), lambda bb, i: (bb, i, 0)),
            pl.BlockSpec((None, POOL_HALO, width), lambda bb, i: (bb, jnp.maximum(i * halo_blocks - 1, 0), 0)),
            pl.BlockSpec((None, POOL_HALO, width),
                         lambda bb, i: (bb, jnp.minimum((i + 1) * halo_blocks, last_halo), 0)),
            pl.BlockSpec((None, tm, width), lambda bb, i: (bb, i, 1)),
            pl.BlockSpec((None, tm, d), lambda bb, i: (bb, i, 0)),
            pl.BlockSpec((None, 1, d), lambda bb, i: (bb, 0, 0)),
            pl.BlockSpec(grp_w.shape, lambda bb, i: (0, 0, 0)),
            pl.BlockSpec((1, width), lambda bb, i: (0, 0)),
            pl.BlockSpec(w_out.shape, lambda bb, i: (0, 0)),
            vec, vec,
        ],
        out_specs=pl.BlockSpec((None, tm, d), lambda bb, i: (bb, i, 0)),
        out_shape=jax.ShapeDtypeStruct((b, s, d), jnp.float32),
        scratch_shapes=[
            pltpu.VMEM((tm + 2 * POOL_HALO, width), jnp.float32),
            pltpu.VMEM((tm, width), jnp.float32),
        ],
        compiler_params=_params("parallel", "parallel"),
        name="pool_mix_output",
    )(ug, ug, ug, ug, x, gate, grp_w, ch_scale, w_out, ln_g, ln_b)


def _rope_tables(s, q_scale):
    t = jnp.arange(s)
    t_row = (t // GRID_W).astype(jnp.float32)
    t_col = (t % GRID_W).astype(jnp.float32)
    inv_freq = ROPE_THETA ** (-jnp.arange(ROPE_PAIRS, dtype=jnp.float32) / ROPE_PAIRS)
    ang_r = t_row[:, None] * inv_freq
    ang_c = t_col[:, None] * inv_freq
    cr, sr, cc, sc = jnp.cos(ang_r), jnp.sin(ang_r), jnp.cos(ang_c), jnp.sin(ang_c)
    z = jnp.zeros_like(sr)
    c = jnp.concatenate([cr, cr, cc, cc], axis=-1)
    sa = jnp.concatenate([-sr, z, -sc, z], axis=-1)
    sb = jnp.concatenate([z, sr, z, sc], axis=-1)
    return tuple(jnp.stack([tab * q_scale, tab]) for tab in (c, sa, sb))


def _identity_tables(s, q_scale):
    one = jnp.ones((s, LANES), jnp.float32)
    zero = jnp.zeros((2, s, LANES), jnp.float32)
    return jnp.stack([one * q_scale, one]), zero, zero


def kernel(x, c, ctx, c_ctx, mod_w, mod_b, ln_g, ln_b, attn_w_in, attn_w_out, attn_lq1, attn_lk1, attn_lq2,
           attn_lk2, attn_subln_g, pool_w_in, pool_grp_w, pool_scale, pool_w_out):
    b, s, d = x.shape
    ctx_len = ctx.shape[1]
    depth = mod_w.shape[0]
    assert d == N_HEADS * V_DIM and b + 1 <= MOD_ROWS
    assert s % PROJ_ROWS == 0 or s < PROJ_ROWS
    alpha = (2 * depth) ** 0.25
    q_scale = LOG2E / math.sqrt(HEAD_DIM)
    bf16 = jnp.bfloat16

    cvec = jnp.concatenate([c, c_ctx[None, :], jnp.zeros((MOD_ROWS - b - 1, d), jnp.float32)], axis=0)
    mod = _modulation(cvec, mod_w, mod_b)
    tabs_x = _rope_tables(s, q_scale)
    tabs_c = _identity_tables(ctx_len, q_scale)

    for i in range(depth):
        is_attn = i % 2 == 0
        ctx_out = any(j % 2 == 0 for j in range(i + 1, depth))
        shift, scale, gate = (mod[i, :, k * d:(k + 1) * d] for k in range(3))
        sh_x, sc_x, gt_x = (v[:b, None, :] for v in (shift, scale, gate))
        sh_c, sc_c, gt_c = (jnp.broadcast_to(v[b][None, None, :], (b, 1, d)) for v in (shift, scale, gate))
        lg, lb = ln_g[i][None, :], ln_b[i][None, :]
        if is_attn:
            a = i // 2
            lam_init = 0.8 - 0.6 * math.exp(-0.3 * i)
            w_in = attn_w_in[a].astype(bf16)
            w_qk, w_vt, w_gate = w_in[:, :2 * d], w_in[:, 2 * d:3 * d].T, w_in[:, 3 * d:]
            w_out = attn_w_out[a].astype(bf16)
            lams = [v[a][None, :] for v in (attn_lq1, attn_lk1, attn_lq2, attn_lk2)]
            subg = attn_subln_g[a][None, :]
            total = s + ctx_len
            qk = _project_qk(x, sh_x, sc_x, w_qk, tabs_x, total, 0)
            qk = _project_qk(ctx, sh_c, sc_c, w_qk, tabs_c, total, s, into=qk)
            vt = _project_vt(x, sh_x, sc_x, w_vt, total, 0)
            vt = _project_vt(ctx, sh_c, sc_c, w_vt, total, s, into=vt)
            g_x = _project_f32(x, sh_x, sc_x, w_gate)
            y_x = _attention(qk, vt, g_x, lams, subg, lam_init, (0, s), (0, total))
            if ctx_out:
                g_c = _project_f32(ctx, sh_c, sc_c, w_gate)
                y_c = _attention(qk, vt, g_c, lams, subg, lam_init, (s, ctx_len), (s, ctx_len))
                ctx = _output_layer_norm(y_c, ctx, gt_c, w_out, lg, lb, alpha)
            x = _output_layer_norm(y_x, x, gt_x, w_out, lg, lb, alpha)
        else:
            p = i // 2
            w_in = pool_w_in[p].astype(bf16)
            grp_w = pool_grp_w[p].astype(bf16)
            w_out = pool_w_out[p].astype(bf16)
            cs = pool_scale[p][None, :]
            ug_x = _project_f32(x, sh_x, sc_x, w_in)
            if ctx_out:
                ug_c = _project_f32(ctx, sh_c, sc_c, w_in)
                ctx = _pool_mix_output(ug_c, ctx, gt_c, grp_w, cs, w_out, lg, lb, alpha)
            x = _pool_mix_output(ug_x, x, gt_x, grp_w, cs, w_out, lg, lb, alpha)
    return x
```

```python
import functools
import math

import jax
import jax.numpy as jnp
from jax import lax
from jax.experimental import pallas as pl
from jax.experimental.pallas import tpu as pltpu

N_HEADS = 8
HEAD_DIM = 128
V_DIM = 2 * HEAD_DIM
GRID_W = 64
ROPE_THETA = 10000.0
ROPE_PAIRS = HEAD_DIM // 4
POOL_WINDOWS = (2, 4, 8, 16)
POOL_HALO = 8
LN_EPS = 1e-5
SUBLN_EPS = 1e-5

LANES = 128
SUBLANES = 8
MXU_DEPTH = 256
VMEM_LIMIT_BYTES = 56 * 1024 * 1024

PROJ_ROWS = 512
PROJ_COLS = 2048
ATTN_Q_ROWS = 2048
ATTN_K_ROWS = 1280
MOD_COLS = 1024
MOD_ROWS = SUBLANES

LOG2E = 1.4426950408889634


def _silu(v):
    return v / (1.0 + jnp.exp(-v))


def _params(*semantics):
    return pltpu.CompilerParams(dimension_semantics=semantics, vmem_limit_bytes=VMEM_LIMIT_BYTES)


def _mod_kernel(c_ref, w_ref, b_ref, o_ref):
    s = _silu(c_ref[...])
    o_ref[...] = jnp.dot(s, w_ref[...], precision=lax.Precision.HIGHEST,
                         preferred_element_type=jnp.float32) + b_ref[...]


def _modulation(cvec, mod_w, mod_b):
    depth, d, n = mod_w.shape
    tn = min(MOD_COLS, n)
    return pl.pallas_call(
        _mod_kernel,
        grid=(depth, n // tn),
        in_specs=[
            pl.BlockSpec((MOD_ROWS, d), lambda l, j: (0, 0)),
            pl.BlockSpec((None, d, tn), lambda l, j: (l, 0, j)),
            pl.BlockSpec((None, 1, tn), lambda l, j: (l, 0, j)),
        ],
        out_specs=pl.BlockSpec((None, MOD_ROWS, tn), lambda l, j: (l, 0, j)),
        out_shape=jax.ShapeDtypeStruct((depth, MOD_ROWS, n), jnp.float32),
        compiler_params=_params("parallel", "parallel"),
        name="modulation",
    )(cvec, mod_w, mod_b.reshape(depth, 1, n))


def _modulated_matmul(x_ref, sh_ref, sc_ref, w_ref):
    h = (x_ref[...] * (1.0 + sc_ref[...]) + sh_ref[...]).astype(jnp.bfloat16)
    return jnp.dot(h, w_ref[...], preferred_element_type=jnp.float32)


def _proj_f32_kernel(x_ref, sh_ref, sc_ref, w_ref, o_ref):
    o_ref[...] = _modulated_matmul(x_ref, sh_ref, sc_ref, w_ref)


def _proj_qk_kernel(x_ref, sh_ref, sc_ref, w_ref, c_ref, sa_ref, sb_ref, *rest):
    o_ref = rest[-1]
    acc = _modulated_matmul(x_ref, sh_ref, sc_ref, w_ref)
    c, sa, sb = c_ref[...], sa_ref[...], sb_ref[...]
    for gi in range(acc.shape[1] // LANES):
        t = acc[:, gi * LANES:(gi + 1) * LANES]
        r = t * c + pltpu.roll(t, LANES - ROPE_PAIRS, 1) * sa + pltpu.roll(t, ROPE_PAIRS, 1) * sb
        head, comp = divmod(gi, V_DIM // LANES)
        o_ref[head, :, comp * LANES:(comp + 1) * LANES] = r.astype(o_ref.dtype)


def _project_qk(x, shift, scale, w_qk, tables, total_rows, row_start, into=None):
    b, s, d = x.shape
    tn = PROJ_COLS
    nt = w_qk.shape[1] // tn
    nh = tn // V_DIM
    tm = min(PROJ_ROWS, s)
    assert row_start % tm == 0
    first = row_start // tm
    tab_spec = pl.BlockSpec((None, tm, LANES), lambda n, bb, i: (n, i, 0))
    in_specs = [
        pl.BlockSpec((None, tm, d), lambda n, bb, i: (bb, i, 0)),
        pl.BlockSpec((None, 1, d), lambda n, bb, i: (bb, 0, 0)),
        pl.BlockSpec((None, 1, d), lambda n, bb, i: (bb, 0, 0)),
        pl.BlockSpec((d, tn), lambda n, bb, i: (0, n)),
        tab_spec, tab_spec, tab_spec,
    ]
    args = [x, shift, scale, w_qk, *tables]
    aliases = {}
    if into is not None:
        aliases = {len(args): 0}
        in_specs.append(pl.BlockSpec(memory_space=pl.ANY))
        args.append(into)
    return pl.pallas_call(
        _proj_qk_kernel,
        grid=(nt, b, s // tm),
        in_specs=in_specs,
        out_specs=pl.BlockSpec((None, None, nh, tm, V_DIM), lambda n, bb, i: (n, bb, 0, first + i, 0)),
        out_shape=jax.ShapeDtypeStruct((nt, b, nh, total_rows, V_DIM), jnp.bfloat16),
        input_output_aliases=aliases,
        compiler_params=_params("arbitrary", "arbitrary", "arbitrary"),
        name="project_qk",
    )(*args)


def _proj_vt_kernel(x_ref, sh_ref, sc_ref, wt_ref, *rest):
    o_ref = rest[-1]
    h = (x_ref[...] * (1.0 + sc_ref[...]) + sh_ref[...]).astype(jnp.bfloat16)
    vt = lax.dot_general(wt_ref[...], h, (((1,), (1,)), ((), ())),
                         preferred_element_type=jnp.float32)
    for head in range(o_ref.shape[0]):
        o_ref[head] = vt[head * V_DIM:(head + 1) * V_DIM, :].astype(o_ref.dtype)


def _project_vt(x, shift, scale, w_vt, total_rows, row_start, into=None):
    b, s, d = x.shape
    nh = w_vt.shape[0] // V_DIM
    tm = min(PROJ_ROWS, s)
    assert row_start % tm == 0
    first = row_start // tm
    in_specs = [
        pl.BlockSpec((None, tm, d), lambda bb, i: (bb, i, 0)),
        pl.BlockSpec((None, 1, d), lambda bb, i: (bb, 0, 0)),
        pl.BlockSpec((None, 1, d), lambda bb, i: (bb, 0, 0)),
        pl.BlockSpec(w_vt.shape, lambda bb, i: (0, 0)),
    ]
    args = [x, shift, scale, w_vt]
    aliases = {}
    if into is not None:
        aliases = {len(args): 0}
        in_specs.append(pl.BlockSpec(memory_space=pl.ANY))
        args.append(into)
    return pl.pallas_call(
        _proj_vt_kernel,
        grid=(b, s // tm),
        in_specs=in_specs,
        out_specs=pl.BlockSpec((None, nh, V_DIM, tm), lambda bb, i: (bb, 0, 0, first + i)),
        out_shape=jax.ShapeDtypeStruct((b, nh, V_DIM, total_rows), jnp.bfloat16),
        input_output_aliases=aliases,
        compiler_params=_params("parallel", "parallel"),
        name="project_vt",
    )(*args)


def _project_f32(x, shift, scale, w):
    b, s, d = x.shape
    tn = PROJ_COLS
    nt = w.shape[1] // tn
    tm = min(PROJ_ROWS, s)
    return pl.pallas_call(
        _proj_f32_kernel,
        grid=(nt, b, s // tm),
        in_specs=[
            pl.BlockSpec((None, tm, d), lambda n, bb, i: (bb, i, 0)),
            pl.BlockSpec((None, 1, d), lambda n, bb, i: (bb, 0, 0)),
            pl.BlockSpec((None, 1, d), lambda n, bb, i: (bb, 0, 0)),
            pl.BlockSpec((d, tn), lambda n, bb, i: (0, n)),
        ],
        out_specs=pl.BlockSpec((None, tm, tn), lambda n, bb, i: (bb, i, n)),
        out_shape=jax.ShapeDtypeStruct((b, s, nt * tn), jnp.float32),
        compiler_params=_params("arbitrary", "arbitrary", "arbitrary"),
        name="project_f32",
    )(x, shift, scale, w)


def _key_block(skv):
    best = MXU_DEPTH
    for cand in range(MXU_DEPTH, min(ATTN_K_ROWS, skv) + 1, MXU_DEPTH):
        if skv % cand == 0:
            best = cand
    return best


def _attn_kernel(lq1_ref, lk1_ref, lq2_ref, lk2_ref, subg_ref, q_ref, k0_ref, kn_ref, vt_ref, g_ref, o_ref,
                 s_sc, mx_sc, m_sc, l_sc, acc_sc, *, lam_init):
    i = pl.program_id(2)
    j = pl.program_id(3)
    tk = kn_ref.shape[0]
    tq = q_ref.shape[0]
    n_query_chunks = tq // MXU_DEPTH

    def scores_piece(kb_ref, comp, qc):
        cols = slice(comp * HEAD_DIM, (comp + 1) * HEAD_DIM)
        qs = slice(qc * MXU_DEPTH, (qc + 1) * MXU_DEPTH)
        st = lax.dot_general(kb_ref[:, cols], q_ref[qs, cols], (((1,), (1,)), ((), ())),
                             preferred_element_type=jnp.float32)
        s_sc[comp, :, qs] = st
        mx_sc[comp, :, qs] = jnp.max(st.reshape(tk // SUBLANES, SUBLANES, MXU_DEPTH), axis=0)

    def block_stats(comp):
        m_blk = jnp.max(mx_sc[comp], axis=0, keepdims=True)
        m_prev = m_sc[comp]
        m_new = jnp.maximum(m_prev, m_blk)
        alpha = jnp.exp2(m_prev - m_new)
        m_sc[comp] = m_new
        return m_new, alpha

    def consume_piece(comp, qc, m_new, alpha):
        qs = slice(qc * MXU_DEPTH, (qc + 1) * MXU_DEPTH)
        st = s_sc[comp, :, qs].reshape(tk // SUBLANES, SUBLANES, MXU_DEPTH)
        p = jnp.exp2(st - jnp.broadcast_to(m_new[:, qs], (SUBLANES, MXU_DEPTH)))
        a = alpha[:, qs]
        l_sc[comp, :, qs] = a * l_sc[comp, :, qs] + jnp.sum(p, axis=0)
        pv = jnp.dot(vt_ref[...], p.reshape(tk, MXU_DEPTH).astype(jnp.bfloat16),
                     preferred_element_type=jnp.float32)
        acc_sc[comp, :, qs] = a * acc_sc[comp, :, qs] + pv

    def step():
        stats = [block_stats(comp) for comp in range(2)]
        order = [(comp, qc) for qc in range(n_query_chunks) for comp in range(2)]
        for n, (comp, qc) in enumerate(order):
            consume_piece(comp, qc, *stats[comp])
            if n > 0:
                scores_piece(kn_ref, *order[n - 1])
        scores_piece(kn_ref, *order[-1])

    @pl.when((i == 0) & (j == 0))
    def _():
        for comp in range(2):
            for qc in range(n_query_chunks):
                scores_piece(k0_ref, comp, qc)

    @pl.when(j == 0)
    def _():
        m_sc[...] = jnp.full(m_sc.shape, -jnp.inf, jnp.float32)
        l_sc[...] = jnp.zeros(l_sc.shape, jnp.float32)
        acc_sc[...] = jnp.zeros(acc_sc.shape, jnp.float32)

    step()


    @pl.when(j == pl.num_programs(3) - 1)
    def _():
        lam = (jnp.exp(jnp.sum(lq1_ref[...] * lk1_ref[...], axis=-1, keepdims=True))
               - jnp.exp(jnp.sum(lq2_ref[...] * lk2_ref[...], axis=-1, keepdims=True)) + lam_init)
        l1 = jnp.sum(l_sc[0], axis=0, keepdims=True)
        l2 = jnp.sum(l_sc[1], axis=0, keepdims=True)
        ot = acc_sc[0] / l1 - lam * (acc_sc[1] / l2)
        ot = ot * lax.rsqrt(jnp.mean(ot * ot, axis=0, keepdims=True) + SUBLN_EPS)
        o = ot.T * subg_ref[...] * (1.0 - lam_init)
        o_ref[...] = (o * _silu(g_ref[...])).astype(o_ref.dtype)


def _attention(qk, vt, gate_path, lams, subln_g, lam_init, q_rows, kv_rows):
    _, b, nh, _, _ = qk.shape
    (q_start, sq), (kv_start, skv) = q_rows, kv_rows
    tq = min(ATTN_Q_ROWS, sq)
    tk = _key_block(skv)
    assert sq % tq == 0 and q_start % tq == 0 and kv_start % tk == 0
    nq, nk = sq // tq, skv // tk
    q0, k0 = q_start // tq, kv_start // tk
    small = pl.BlockSpec((1, HEAD_DIM), lambda bb, h, i, j: (0, 0))
    return pl.pallas_call(
        functools.partial(_attn_kernel, lam_init=lam_init),
        grid=(b, nh, nq, nk),
        in_specs=[
            small, small, small, small,
            pl.BlockSpec((1, V_DIM), lambda bb, h, i, j: (0, 0)),
            pl.BlockSpec((None, None, None, tq, V_DIM),
                         lambda bb, h, i, j: (0, bb, h, q0 + jnp.minimum(i + (j + 1) // nk, nq - 1), 0)),
            pl.BlockSpec((None, None, None, tk, V_DIM), lambda bb, h, i, j: (1, bb, h, k0, 0)),
            pl.BlockSpec((None, None, None, tk, V_DIM), lambda bb, h, i, j: (1, bb, h, k0 + (j + 1) % nk, 0)),
            pl.BlockSpec((None, None, V_DIM, tk), lambda bb, h, i, j: (bb, h, 0, k0 + j)),
            pl.BlockSpec((None, tq, V_DIM), lambda bb, h, i, j: (bb, i, h)),
        ],
        out_specs=pl.BlockSpec((None, tq, V_DIM), lambda bb, h, i, j: (bb, i, h)),
        out_shape=jax.ShapeDtypeStruct((b, sq, nh * V_DIM), jnp.bfloat16),
        scratch_shapes=[
            pltpu.VMEM((2, tk, tq), jnp.float32),
            pltpu.VMEM((2, SUBLANES, tq), jnp.float32),
            pltpu.VMEM((2, 1, tq), jnp.float32),
            pltpu.VMEM((2, SUBLANES, tq), jnp.float32),
            pltpu.VMEM((2, V_DIM, tq), jnp.float32),
        ],
        compiler_params=_params("parallel", "parallel", "arbitrary", "arbitrary"),
        name="diff_attention",
    )(*lams, subln_g, qk, qk, qk, vt, gate_path)


def _residual_layer_norm(x, gate, y, ln_g, ln_b, alpha):
    z = alpha * x + gate * y
    mu = jnp.mean(z, axis=-1, keepdims=True)
    zc = z - mu
    var = jnp.mean(zc * zc, axis=-1, keepdims=True)
    return zc * lax.rsqrt(var + LN_EPS) * ln_g + ln_b


def _out_kernel(y_ref, x_ref, gate_ref, w_ref, lg_ref, lb_ref, o_ref, *, alpha):
    y = jnp.dot(y_ref[...], w_ref[...], preferred_element_type=jnp.float32)
    o_ref[...] = _residual_layer_norm(x_ref[...], gate_ref[...], y, lg_ref[...], lb_ref[...], alpha)


def _output_layer_norm(y, x, gate, w_out, ln_g, ln_b, alpha):
    b, s, d = x.shape
    tm = min(PROJ_ROWS, s)
    vec = pl.BlockSpec((1, d), lambda bb, i: (0, 0))
    return pl.pallas_call(
        functools.partial(_out_kernel, alpha=alpha),
        grid=(b, s // tm),
        in_specs=[
            pl.BlockSpec((None, tm, y.shape[2]), lambda bb, i: (bb, i, 0)),
            pl.BlockSpec((None, tm, d), lambda bb, i: (bb, i, 0)),
            pl.BlockSpec((None, 1, d), lambda bb, i: (bb, 0, 0)),
            pl.BlockSpec(w_out.shape, lambda bb, i: (0, 0)),
            vec, vec,
        ],
        out_specs=pl.BlockSpec((None, tm, d), lambda bb, i: (bb, i, 0)),
        out_shape=jax.ShapeDtypeStruct((b, s, d), jnp.float32),
        compiler_params=_params("parallel", "parallel"),
        name="output_layer_norm",
    )(y, x, gate, w_out, ln_g, ln_b)


def _pool_kernel(u_ref, up_ref, un_ref, g_ref, x_ref, gate_ref, gw_ref, cs_ref, w_ref, lg_ref, lb_ref,
                 o_ref, ubuf, mix, *, seq, alpha):
    i = pl.program_id(1)
    tm = u_ref.shape[0]
    gdim = u_ref.shape[1] // len(POOL_WINDOWS)
    ubuf[0:POOL_HALO, :] = jnp.where(i > 0, up_ref[...], 0.0)
    ubuf[POOL_HALO:POOL_HALO + tm, :] = u_ref[...]
    ubuf[POOL_HALO + tm:2 * POOL_HALO + tm, :] = jnp.where(i < pl.num_programs(1) - 1, un_ref[...], 0.0)
    t = i * tm + lax.broadcasted_iota(jnp.int32, (tm, 1), 0)
    for gi, w in enumerate(POOL_WINDOWS):
        lo = w // 2
        hi = w - 1 - lo
        cols = slice(gi * gdim, (gi + 1) * gdim)
        win = ubuf[POOL_HALO - lo:POOL_HALO - lo + tm, cols]
        for off in range(-lo + 1, hi + 1):
            win = win + ubuf[POOL_HALO + off:POOL_HALO + off + tm, cols]
        cnt = (jnp.minimum(t + hi + 1, seq) - jnp.maximum(t - lo, 0)).astype(jnp.float32)
        pooled = win / cnt - u_ref[:, cols]
        mix[:, cols] = jnp.dot(pooled.astype(jnp.bfloat16), gw_ref[gi], preferred_element_type=jnp.float32)
    gated = ((mix[...] * cs_ref[...]) * _silu(g_ref[...])).astype(jnp.bfloat16)
    y = jnp.dot(gated, w_ref[...], preferred_element_type=jnp.float32)
    o_ref[...] = _residual_layer_norm(x_ref[...], gate_ref[...], y, lg_ref[...], lb_ref[...], alpha)


def _pool_mix_output(ug, x, gate, grp_w, ch_scale, w_out, ln_g, ln_b, alpha):
    b, s, d = x.shape
    width = ug.shape[2] // 2
    tm = min(PROJ_ROWS, s)
    halo_blocks = tm // POOL_HALO
    last_halo = s // POOL_HALO - 1
    vec = pl.BlockSpec((1, d), lambda bb, i: (0, 0))
    return pl.pallas_call(
        functools.partial(_pool_kernel, seq=s, alpha=alpha),
        grid=(b, s // tm),
        in_specs=[
            pl.BlockSpec((None, tm, width), lambda bb, i: (bb, i, 0)),
            pl.BlockSpec((None, POOL_HALO, width), lambda bb, i: (bb, jnp.maximum(i * halo_blocks - 1, 0), 0)),
            pl.BlockSpec((None, POOL_HALO, width),
                         lambda bb, i: (bb, jnp.minimum((i + 1) * halo_blocks, last_halo), 0)),
            pl.BlockSpec((None, tm, width), lambda bb, i: (bb, i, 1)),
            pl.BlockSpec((None, tm, d), lambda bb, i: (bb, i, 0)),
            pl.BlockSpec((None, 1, d), lambda bb, i: (bb, 0, 0)),
            pl.BlockSpec(grp_w.shape, lambda bb, i: (0, 0, 0)),
            pl.BlockSpec((1, width), lambda bb, i: (0, 0)),
            pl.BlockSpec(w_out.shape, lambda bb, i: (0, 0)),
            vec, vec,
        ],
        out_specs=pl.BlockSpec((None, tm, d), lambda bb, i: (bb, i, 0)),
        out_shape=jax.ShapeDtypeStruct((b, s, d), jnp.float32),
        scratch_shapes=[
            pltpu.VMEM((tm + 2 * POOL_HALO, width), jnp.float32),
            pltpu.VMEM((tm, width), jnp.float32),
        ],
        compiler_params=_params("parallel", "parallel"),
        name="pool_mix_output",
    )(ug, ug, ug, ug, x, gate, grp_w, ch_scale, w_out, ln_g, ln_b)


def _rope_tables(s, q_scale):
    t = jnp.arange(s)
    t_row = (t // GRID_W).astype(jnp.float32)
    t_col = (t % GRID_W).astype(jnp.float32)
    inv_freq = ROPE_THETA ** (-jnp.arange(ROPE_PAIRS, dtype=jnp.float32) / ROPE_PAIRS)
    ang_r = t_row[:, None] * inv_freq
    ang_c = t_col[:, None] * inv_freq
    cr, sr, cc, sc = jnp.cos(ang_r), jnp.sin(ang_r), jnp.cos(ang_c), jnp.sin(ang_c)
    z = jnp.zeros_like(sr)
    c = jnp.concatenate([cr, cr, cc, cc], axis=-1)
    sa = jnp.concatenate([-sr, z, -sc, z], axis=-1)
    sb = jnp.concatenate([z, sr, z, sc], axis=-1)
    return tuple(jnp.stack([tab * q_scale, tab]) for tab in (c, sa, sb))


def _identity_tables(s, q_scale):
    one = jnp.ones((s, LANES), jnp.float32)
    zero = jnp.zeros((2, s, LANES), jnp.float32)
    return jnp.stack([one * q_scale, one]), zero, zero


def kernel(x, c, ctx, c_ctx, mod_w, mod_b, ln_g, ln_b, attn_w_in, attn_w_out, attn_lq1, attn_lk1, attn_lq2,
           attn_lk2, attn_subln_g, pool_w_in, pool_grp_w, pool_scale, pool_w_out):
    b, s, d = x.shape
    ctx_len = ctx.shape[1]
    depth = mod_w.shape[0]
    assert d == N_HEADS * V_DIM and b + 1 <= MOD_ROWS
    assert s % PROJ_ROWS == 0 or s < PROJ_ROWS
    alpha = (2 * depth) ** 0.25
    q_scale = LOG2E / math.sqrt(HEAD_DIM)
    bf16 = jnp.bfloat16

    cvec = jnp.concatenate([c, c_ctx[None, :], jnp.zeros((MOD_ROWS - b - 1, d), jnp.float32)], axis=0)
    mod = _modulation(cvec, mod_w, mod_b)
    tabs_x = _rope_tables(s, q_scale)
    tabs_c = _identity_tables(ctx_len, q_scale)

    for i in range(depth):
        is_attn = i % 2 == 0
        ctx_out = any(j % 2 == 0 for j in range(i + 1, depth))
        shift, scale, gate = (mod[i, :, k * d:(k + 1) * d] for k in range(3))
        sh_x, sc_x, gt_x = (v[:b, None, :] for v in (shift, scale, gate))
        sh_c, sc_c, gt_c = (jnp.broadcast_to(v[b][None, None, :], (b, 1, d)) for v in (shift, scale, gate))
        lg, lb = ln_g[i][None, :], ln_b[i][None, :]
        if is_attn:
            a = i // 2
            lam_init = 0.8 - 0.6 * math.exp(-0.3 * i)
            w_in = attn_w_in[a].astype(bf16)
            w_qk, w_vt, w_gate = w_in[:, :2 * d], w_in[:, 2 * d:3 * d].T, w_in[:, 3 * d:]
            w_out = attn_w_out[a].astype(bf16)
            lams = [v[a][None, :] for v in (attn_lq1, attn_lk1, attn_lq2, attn_lk2)]
            subg = attn_subln_g[a][None, :]
            total = s + ctx_len
            qk = _project_qk(x, sh_x, sc_x, w_qk, tabs_x, total, 0)
            qk = _project_qk(ctx, sh_c, sc_c, w_qk, tabs_c, total, s, into=qk)
            vt = _project_vt(x, sh_x, sc_x, w_vt, total, 0)
            vt = _project_vt(ctx, sh_c, sc_c, w_vt, total, s, into=vt)
            g_x = _project_f32(x, sh_x, sc_x, w_gate)
            y_x = _attention(qk, vt, g_x, lams, subg, lam_init, (0, s), (0, total))
            if ctx_out:
                g_c = _project_f32(ctx, sh_c, sc_c, w_gate)
                y_c = _attention(qk, vt, g_c, lams, subg, lam_init, (s, ctx_len), (s, ctx_len))
                ctx = _output_layer_norm(y_c, ctx, gt_c, w_out, lg, lb, alpha)
            x = _output_layer_norm(y_x, x, gt_x, w_out, lg, lb, alpha)
        else:
            p = i // 2
            w_in = pool_w_in[p].astype(bf16)
            grp_w = pool_grp_w[p].astype(bf16)
            w_out = pool_w_out[p].astype(bf16)
            cs = pool_scale[p][None, :]
            ug_x = _project_f32(x, sh_x, sc_x, w_in)
            if ctx_out:
                ug_c = _project_f32(ctx, sh_c, sc_c, w_in)
                ctx = _pool_mix_output(ug_c, ctx, gt_c, grp_w, cs, w_out, lg, lb, alpha)
            x = _pool_mix_output(ug_x, x, gt_x, grp_w, cs, w_out, lg, lb, alpha)
    return x
```

```python
import functools
import math

import jax
import jax.numpy as jnp
from jax import lax
from jax.experimental import pallas as pl
from jax.experimental.pallas import tpu as pltpu

N_HEADS = 8
HEAD_DIM = 128
V_DIM = 2 * HEAD_DIM
GRID_W = 64
ROPE_THETA = 10000.0
ROPE_PAIRS = HEAD_DIM // 4
POOL_WINDOWS = (2, 4, 8, 16)
POOL_HALO = 8
LN_EPS = 1e-5
SUBLN_EPS = 1e-5

LANES = 128
SUBLANES = 8
MXU_DEPTH = 256
VMEM_LIMIT_BYTES = 56 * 1024 * 1024

PROJ_ROWS = 512
PROJ_COLS = 2048
ATTN_Q_ROWS = 2048
ATTN_K_ROWS = 1280
MOD_COLS = 1024
MOD_ROWS = SUBLANES

LOG2E = 1.4426950408889634


def _silu(v):
    return v / (1.0 + jnp.exp(-v))


def _params(*semantics):
    return pltpu.CompilerParams(dimension_semantics=semantics, vmem_limit_bytes=VMEM_LIMIT_BYTES)


def _mod_kernel(c_ref, w_ref, b_ref, o_ref):
    s = _silu(c_ref[...])
    o_ref[...] = jnp.dot(s, w_ref[...], precision=lax.Precision.HIGHEST,
                         preferred_element_type=jnp.float32) + b_ref[...]


def _modulation(cvec, mod_w, mod_b):
    depth, d, n = mod_w.shape
    tn = min(MOD_COLS, n)
    return pl.pallas_call(
        _mod_kernel,
        grid=(depth, n // tn),
        in_specs=[
            pl.BlockSpec((MOD_ROWS, d), lambda l, j: (0, 0)),
            pl.BlockSpec((None, d, tn), lambda l, j: (l, 0, j)),
            pl.BlockSpec((None, 1, tn), lambda l, j: (l, 0, j)),
        ],
        out_specs=pl.BlockSpec((None, MOD_ROWS, tn), lambda l, j: (l, 0, j)),
        out_shape=jax.ShapeDtypeStruct((depth, MOD_ROWS, n), jnp.float32),
        compiler_params=_params("parallel", "parallel"),
        name="modulation",
    )(cvec, mod_w, mod_b.reshape(depth, 1, n))


def _modulated_matmul(x_ref, sh_ref, sc_ref, w_ref):
    h = (x_ref[...] * (1.0 + sc_ref[...]) + sh_ref[...]).astype(jnp.bfloat16)
    return jnp.dot(h, w_ref[...], preferred_element_type=jnp.float32)


def _proj_f32_kernel(x_ref, sh_ref, sc_ref, w_ref, o_ref):
    o_ref[...] = _modulated_matmul(x_ref, sh_ref, sc_ref, w_ref)


def _stage_tokens(h_sc, x_ref, ctx_ref, sh_ref, sc_ref, step, n_latent_steps):
    scale1 = 1.0 + sc_ref[...]
    shift = sh_ref[...]

    @pl.when(step < n_latent_steps)
    def _():
        h_sc[...] = (x_ref[...] * scale1 + shift).astype(h_sc.dtype)

    @pl.when(step == n_latent_steps)
    def _():
        n_ctx = ctx_ref.shape[0]
        h_sc[0:n_ctx, :] = (ctx_ref[...] * scale1 + shift).astype(h_sc.dtype)
        if n_ctx < h_sc.shape[0]:
            h_sc[n_ctx:, :] = jnp.zeros((h_sc.shape[0] - n_ctx, h_sc.shape[1]), h_sc.dtype)


def _token_specs(tm, d, ctx_len, n_latent_steps, grid_rank):
    def lift(fn):
        return (lambda n, bb, i: fn(bb, i)) if grid_rank == 3 else fn
    return [
        pl.BlockSpec((None, tm, d), lift(lambda bb, i: (bb, jnp.minimum(i, n_latent_steps - 1), 0))),
        pl.BlockSpec((None, ctx_len, d), lift(lambda bb, i: (bb, 0, 0))),
        pl.BlockSpec((None, None, 1, d), lift(lambda bb, i: (bb, i // n_latent_steps, 0, 0))),
        pl.BlockSpec((None, None, 1, d), lift(lambda bb, i: (bb, i // n_latent_steps, 0, 0))),
    ]


def _proj_qk_kernel(x_ref, ctx_ref, sh_ref, sc_ref, w_ref, c_ref, sa_ref, sb_ref, o_ref, h_sc):
    _stage_tokens(h_sc, x_ref, ctx_ref, sh_ref, sc_ref, pl.program_id(2), pl.num_programs(2) - 1)
    acc = jnp.dot(h_sc[...], w_ref[...], preferred_element_type=jnp.float32)
    c, sa, sb = c_ref[...], sa_ref[...], sb_ref[...]
    for gi in range(acc.shape[1] // LANES):
        t = acc[:, gi * LANES:(gi + 1) * LANES]
        r = t * c + pltpu.roll(t, LANES - ROPE_PAIRS, 1) * sa + pltpu.roll(t, ROPE_PAIRS, 1) * sb
        head, comp = divmod(gi, V_DIM // LANES)
        o_ref[head, :, comp * LANES:(comp + 1) * LANES] = r.astype(o_ref.dtype)


def _project_qk(x, ctx, shift, scale, w_qk, tables):
    b, s, d = x.shape
    ctx_len = ctx.shape[1]
    tn = PROJ_COLS
    nt = w_qk.shape[1] // tn
    nh = tn // V_DIM
    tm = PROJ_ROWS
    nx = s // tm
    tab_spec = pl.BlockSpec((None, tm, LANES), lambda n, bb, i: (n, i, 0))
    return pl.pallas_call(
        _proj_qk_kernel,
        grid=(nt, b, nx + 1),
        in_specs=_token_specs(tm, d, ctx_len, nx, 3) + [
            pl.BlockSpec((d, tn), lambda n, bb, i: (0, n)),
            tab_spec, tab_spec, tab_spec,
        ],
        out_specs=pl.BlockSpec((None, None, nh, tm, V_DIM), lambda n, bb, i: (n, bb, 0, i, 0)),
        out_shape=jax.ShapeDtypeStruct((nt, b, nh, (nx + 1) * tm, V_DIM), jnp.bfloat16),
        scratch_shapes=[pltpu.VMEM((tm, d), jnp.bfloat16)],
        compiler_params=_params("arbitrary", "arbitrary", "arbitrary"),
        name="project_qk",
    )(x, ctx, shift, scale, w_qk, *tables)


def _proj_vt_kernel(x_ref, ctx_ref, sh_ref, sc_ref, wt_ref, o_ref, h_sc):
    _stage_tokens(h_sc, x_ref, ctx_ref, sh_ref, sc_ref, pl.program_id(1), pl.num_programs(1) - 1)
    vt = lax.dot_general(wt_ref[...], h_sc[...], (((1,), (1,)), ((), ())),
                         preferred_element_type=jnp.float32)
    for head in range(o_ref.shape[0]):
        o_ref[head] = vt[head * V_DIM:(head + 1) * V_DIM, :].astype(o_ref.dtype)


def _project_vt(x, ctx, shift, scale, w_vt):
    b, s, d = x.shape
    ctx_len = ctx.shape[1]
    nh = w_vt.shape[0] // V_DIM
    tm = PROJ_ROWS
    nx = s // tm
    return pl.pallas_call(
        _proj_vt_kernel,
        grid=(b, nx + 1),
        in_specs=_token_specs(tm, d, ctx_len, nx, 2) + [pl.BlockSpec(w_vt.shape, lambda bb, i: (0, 0))],
        out_specs=pl.BlockSpec((None, nh, V_DIM, tm), lambda bb, i: (bb, 0, 0, i)),
        out_shape=jax.ShapeDtypeStruct((b, nh, V_DIM, (nx + 1) * tm), jnp.bfloat16),
        scratch_shapes=[pltpu.VMEM((tm, d), jnp.bfloat16)],
        compiler_params=_params("parallel", "arbitrary"),
        name="project_vt",
    )(x, ctx, shift, scale, w_vt)


def _project_f32(x, shift, scale, w):
    b, s, d = x.shape
    tn = PROJ_COLS
    nt = w.shape[1] // tn
    tm = min(PROJ_ROWS, s)
    return pl.pallas_call(
        _proj_f32_kernel,
        grid=(nt, b, s // tm),
        in_specs=[
            pl.BlockSpec((None, tm, d), lambda n, bb, i: (bb, i, 0)),
            pl.BlockSpec((None, 1, d), lambda n, bb, i: (bb, 0, 0)),
            pl.BlockSpec((None, 1, d), lambda n, bb, i: (bb, 0, 0)),
            pl.BlockSpec((d, tn), lambda n, bb, i: (0, n)),
        ],
        out_specs=pl.BlockSpec((None, tm, tn), lambda n, bb, i: (bb, i, n)),
        out_shape=jax.ShapeDtypeStruct((b, s, nt * tn), jnp.float32),
        compiler_params=_params("arbitrary", "arbitrary", "arbitrary"),
        name="project_f32",
    )(x, shift, scale, w)


def _key_block(skv):
    best = MXU_DEPTH
    for cand in range(MXU_DEPTH, min(ATTN_K_ROWS, skv) + 1, MXU_DEPTH):
        if skv % cand == 0:
            best = cand
    return best


def _attn_kernel(lq1_ref, lk1_ref, lq2_ref, lk2_ref, subg_ref, q_ref, k0_ref, kn_ref, vt_ref, g_ref, o_ref,
                 s_sc, mx_sc, m_sc, l_sc, acc_sc, *, lam_init):
    i = pl.program_id(2)
    j = pl.program_id(3)
    tk = kn_ref.shape[0]
    tq = q_ref.shape[0]
    n_query_chunks = tq // MXU_DEPTH

    def scores_piece(kb_ref, comp, qc):
        cols = slice(comp * HEAD_DIM, (comp + 1) * HEAD_DIM)
        qs = slice(qc * MXU_DEPTH, (qc + 1) * MXU_DEPTH)
        st = lax.dot_general(kb_ref[:, cols], q_ref[qs, cols], (((1,), (1,)), ((), ())),
                             preferred_element_type=jnp.float32)
        s_sc[comp, :, qs] = st
        mx_sc[comp, :, qs] = jnp.max(st.reshape(tk // SUBLANES, SUBLANES, MXU_DEPTH), axis=0)

    def block_stats(comp):
        m_blk = jnp.max(mx_sc[comp], axis=0, keepdims=True)
        m_prev = m_sc[comp]
        m_new = jnp.maximum(m_prev, m_blk)
        alpha = jnp.exp2(m_prev - m_new)
        m_sc[comp] = m_new
        return m_new, alpha

    def consume_piece(comp, qc, m_new, alpha):
        qs = slice(qc * MXU_DEPTH, (qc + 1) * MXU_DEPTH)
        st = s_sc[comp, :, qs].reshape(tk // SUBLANES, SUBLANES, MXU_DEPTH)
        p = jnp.exp2(st - jnp.broadcast_to(m_new[:, qs], (SUBLANES, MXU_DEPTH)))
        a = alpha[:, qs]
        l_sc[comp, :, qs] = a * l_sc[comp, :, qs] + jnp.sum(p, axis=0)
        pv = jnp.dot(vt_ref[...], p.reshape(tk, MXU_DEPTH).astype(jnp.bfloat16),
                     preferred_element_type=jnp.float32)
        acc_sc[comp, :, qs] = a * acc_sc[comp, :, qs] + pv

    def step():
        stats = [block_stats(comp) for comp in range(2)]
        order = [(comp, qc) for qc in range(n_query_chunks) for comp in range(2)]
        for n, (comp, qc) in enumerate(order):
            consume_piece(comp, qc, *stats[comp])
            if n > 0:
                scores_piece(kn_ref, *order[n - 1])
        scores_piece(kn_ref, *order[-1])

    @pl.when((i == 0) & (j == 0))
    def _():
        for comp in range(2):
            for qc in range(n_query_chunks):
                scores_piece(k0_ref, comp, qc)

    @pl.when(j == 0)
    def _():
        m_sc[...] = jnp.full(m_sc.shape, -jnp.inf, jnp.float32)
        l_sc[...] = jnp.zeros(l_sc.shape, jnp.float32)
        acc_sc[...] = jnp.zeros(acc_sc.shape, jnp.float32)

    step()


    @pl.when(j == pl.num_programs(3) - 1)
    def _():
        lam = (jnp.exp(jnp.sum(lq1_ref[...] * lk1_ref[...], axis=-1, keepdims=True))
               - jnp.exp(jnp.sum(lq2_ref[...] * lk2_ref[...], axis=-1, keepdims=True)) + lam_init)
        l1 = jnp.sum(l_sc[0], axis=0, keepdims=True)
        l2 = jnp.sum(l_sc[1], axis=0, keepdims=True)
        ot = acc_sc[0] / l1 - lam * (acc_sc[1] / l2)
        ot = ot * lax.rsqrt(jnp.mean(ot * ot, axis=0, keepdims=True) + SUBLN_EPS)
        o = ot.T * subg_ref[...] * (1.0 - lam_init)
        o_ref[...] = (o * _silu(g_ref[...])).astype(o_ref.dtype)


def _attention(qk, vt, gate_path, lams, subln_g, lam_init, q_rows, kv_rows):
    _, b, nh, _, _ = qk.shape
    (q_start, sq), (kv_start, skv) = q_rows, kv_rows
    tq = min(ATTN_Q_ROWS, sq)
    tk = _key_block(skv)
    assert sq % tq == 0 and q_start % tq == 0 and kv_start % tk == 0
    nq, nk = sq // tq, skv // tk
    q0, k0 = q_start // tq, kv_start // tk
    small = pl.BlockSpec((1, HEAD_DIM), lambda bb, h, i, j: (0, 0))
    return pl.pallas_call(
        functools.partial(_attn_kernel, lam_init=lam_init),
        grid=(b, nh, nq, nk),
        in_specs=[
            small, small, small, small,
            pl.BlockSpec((1, V_DIM), lambda bb, h, i, j: (0, 0)),
            pl.BlockSpec((None, None, None, tq, V_DIM),
                         lambda bb, h, i, j: (0, bb, h, q0 + jnp.minimum(i + (j + 1) // nk, nq - 1), 0)),
            pl.BlockSpec((None, None, None, tk, V_DIM), lambda bb, h, i, j: (1, bb, h, k0, 0)),
            pl.BlockSpec((None, None, None, tk, V_DIM), lambda bb, h, i, j: (1, bb, h, k0 + (j + 1) % nk, 0)),
            pl.BlockSpec((None, None, V_DIM, tk), lambda bb, h, i, j: (bb, h, 0, k0 + j)),
            pl.BlockSpec((None, tq, V_DIM), lambda bb, h, i, j: (bb, i, h)),
        ],
        out_specs=pl.BlockSpec((None, tq, V_DIM), lambda bb, h, i, j: (bb, i, h)),
        out_shape=jax.ShapeDtypeStruct((b, sq, nh * V_DIM), jnp.bfloat16),
        scratch_shapes=[
            pltpu.VMEM((2, tk, tq), jnp.float32),
            pltpu.VMEM((2, SUBLANES, tq), jnp.float32),
            pltpu.VMEM((2, 1, tq), jnp.float32),
            pltpu.VMEM((2, SUBLANES, tq), jnp.float32),
            pltpu.VMEM((2, V_DIM, tq), jnp.float32),
        ],
        compiler_params=_params("parallel", "parallel", "arbitrary", "arbitrary"),
        name="diff_attention",
    )(*lams, subln_g, qk, qk, qk, vt, gate_path)


def _residual_layer_norm(x, gate, y, ln_g, ln_b, alpha):
    z = alpha * x + gate * y
    mu = jnp.mean(z, axis=-1, keepdims=True)
    zc = z - mu
    var = jnp.mean(zc * zc, axis=-1, keepdims=True)
    return zc * lax.rsqrt(var + LN_EPS) * ln_g + ln_b


def _out_kernel(y_ref, x_ref, gate_ref, w_ref, lg_ref, lb_ref, o_ref, *, alpha):
    y = jnp.dot(y_ref[...], w_ref[...], preferred_element_type=jnp.float32)
    o_ref[...] = _residual_layer_norm(x_ref[...], gate_ref[...], y, lg_ref[...], lb_ref[...], alpha)


def _output_layer_norm(y, x, gate, w_out, ln_g, ln_b, alpha):
    b, s, d = x.shape
    tm = min(PROJ_ROWS, s)
    vec = pl.BlockSpec((1, d), lambda bb, i: (0, 0))
    return pl.pallas_call(
        functools.partial(_out_kernel, alpha=alpha),
        grid=(b, s // tm),
        in_specs=[
            pl.BlockSpec((None, tm, y.shape[2]), lambda bb, i: (bb, i, 0)),
            pl.BlockSpec((None, tm, d), lambda bb, i: (bb, i, 0)),
            pl.BlockSpec((None, 1, d), lambda bb, i: (bb, 0, 0)),
            pl.BlockSpec(w_out.shape, lambda bb, i: (0, 0)),
            vec, vec,
        ],
        out_specs=pl.BlockSpec((None, tm, d), lambda bb, i: (bb, i, 0)),
        out_shape=jax.ShapeDtypeStruct((b, s, d), jnp.float32),
        compiler_params=_params("parallel", "parallel"),
        name="output_layer_norm",
    )(y, x, gate, w_out, ln_g, ln_b)


def _pool_kernel(u_ref, up_ref, un_ref, g_ref, x_ref, gate_ref, gw_ref, cs_ref, w_ref, lg_ref, lb_ref,
                 o_ref, ubuf, mix, *, seq, alpha):
    i = pl.program_id(1)
    tm = u_ref.shape[0]
    gdim = u_ref.shape[1] // len(POOL_WINDOWS)
    ubuf[0:POOL_HALO, :] = jnp.where(i > 0, up_ref[...], 0.0)
    ubuf[POOL_HALO:POOL_HALO + tm, :] = u_ref[...]
    ubuf[POOL_HALO + tm:2 * POOL_HALO + tm, :] = jnp.where(i < pl.num_programs(1) - 1, un_ref[...], 0.0)
    t = i * tm + lax.broadcasted_iota(jnp.int32, (tm, 1), 0)
    for gi, w in enumerate(POOL_WINDOWS):
        lo = w // 2
        hi = w - 1 - lo
        cols = slice(gi * gdim, (gi + 1) * gdim)
        win = ubuf[POOL_HALO - lo:POOL_HALO - lo + tm, cols]
        for off in range(-lo + 1, hi + 1):
            win = win + ubuf[POOL_HALO + off:POOL_HALO + off + tm, cols]
        cnt = (jnp.minimum(t + hi + 1, seq) - jnp.maximum(t - lo, 0)).astype(jnp.float32)
        pooled = win / cnt - u_ref[:, cols]
        mix[:, cols] = jnp.dot(pooled.astype(jnp.bfloat16), gw_ref[gi], preferred_element_type=jnp.float32)
    gated = ((mix[...] * cs_ref[...]) * _silu(g_ref[...])).astype(jnp.bfloat16)
    y = jnp.dot(gated, w_ref[...], preferred_element_type=jnp.float32)
    o_ref[...] = _residual_layer_norm(x_ref[...], gate_ref[...], y, lg_ref[...], lb_ref[...], alpha)


def _pool_mix_output(ug, x, gate, grp_w, ch_scale, w_out, ln_g, ln_b, alpha):
    b, s, d = x.shape
    width = ug.shape[2] // 2
    tm = min(PROJ_ROWS, s)
    halo_blocks = tm // POOL_HALO
    last_halo = s // POOL_HALO - 1
    vec = pl.BlockSpec((1, d), lambda bb, i: (0, 0))
    return pl.pallas_call(
        functools.partial(_pool_kernel, seq=s, alpha=alpha),
        grid=(b, s // tm),
        in_specs=[
            pl.BlockSpec((None, tm, width), lambda bb, i: (bb, i, 0)),
            pl.BlockSpec((None, POOL_HALO, width), lambda bb, i: (bb, jnp.maximum(i * halo_blocks - 1, 0), 0)),
            pl.BlockSpec((None, POOL_HALO, width),
                         lambda bb, i: (bb, jnp.minimum((i + 1) * halo_blocks, last_halo), 0)),
            pl.BlockSpec((None, tm, width), lambda bb, i: (bb, i, 1)),
            pl.BlockSpec((None, tm, d), lambda bb, i: (bb, i, 0)),
            pl.BlockSpec((None, 1, d), lambda bb, i: (bb, 0, 0)),
            pl.BlockSpec(grp_w.shape, lambda bb, i: (0, 0, 0)),
            pl.BlockSpec((1, width), lambda bb, i: (0, 0)),
            pl.BlockSpec(w_out.shape, lambda bb, i: (0, 0)),
            vec, vec,
        ],
        out_specs=pl.BlockSpec((None, tm, d), lambda bb, i: (bb, i, 0)),
        out_shape=jax.ShapeDtypeStruct((b, s, d), jnp.float32),
        scratch_shapes=[
            pltpu.VMEM((tm + 2 * POOL_HALO, width), jnp.float32),
            pltpu.VMEM((tm, width), jnp.float32),
        ],
        compiler_params=_params("parallel", "parallel"),
        name="pool_mix_output",
    )(ug, ug, ug, ug, x, gate, grp_w, ch_scale, w_out, ln_g, ln_b)


def _rope_tables(s, total_rows, q_scale):
    t = jnp.arange(s)
    t_row = (t // GRID_W).astype(jnp.float32)
    t_col = (t % GRID_W).astype(jnp.float32)
    inv_freq = ROPE_THETA ** (-jnp.arange(ROPE_PAIRS, dtype=jnp.float32) / ROPE_PAIRS)
    ang_r = t_row[:, None] * inv_freq
    ang_c = t_col[:, None] * inv_freq
    cr, sr, cc, sc = jnp.cos(ang_r), jnp.sin(ang_r), jnp.cos(ang_c), jnp.sin(ang_c)
    z = jnp.zeros_like(sr)
    c = jnp.concatenate([cr, cr, cc, cc], axis=-1)
    sa = jnp.concatenate([-sr, z, -sc, z], axis=-1)
    sb = jnp.concatenate([z, sr, z, sc], axis=-1)
    rest = total_rows - s
    c = jnp.concatenate([c, jnp.ones((rest, LANES), jnp.float32)], axis=0)
    sa, sb = (jnp.concatenate([tab, jnp.zeros((rest, LANES), jnp.float32)], axis=0) for tab in (sa, sb))
    return tuple(jnp.stack([tab * q_scale, tab]) for tab in (c, sa, sb))


def kernel(x, c, ctx, c_ctx, mod_w, mod_b, ln_g, ln_b, attn_w_in, attn_w_out, attn_lq1, attn_lk1, attn_lq2,
           attn_lk2, attn_subln_g, pool_w_in, pool_grp_w, pool_scale, pool_w_out):
    b, s, d = x.shape
    ctx_len = ctx.shape[1]
    depth = mod_w.shape[0]
    assert d == N_HEADS * V_DIM and b + 1 <= MOD_ROWS
    assert s % PROJ_ROWS == 0 and ctx_len <= PROJ_ROWS
    alpha = (2 * depth) ** 0.25
    q_scale = LOG2E / math.sqrt(HEAD_DIM)
    bf16 = jnp.bfloat16

    cvec = jnp.concatenate([c, c_ctx[None, :], jnp.zeros((MOD_ROWS - b - 1, d), jnp.float32)], axis=0)
    mod = _modulation(cvec, mod_w, mod_b)
    tabs = _rope_tables(s, s + PROJ_ROWS, q_scale)

    for i in range(depth):
        is_attn = i % 2 == 0
        ctx_out = any(j % 2 == 0 for j in range(i + 1, depth))
        shift, scale, gate = (mod[i, :, k * d:(k + 1) * d] for k in range(3))
        sh_x, sc_x, gt_x = (v[:b, None, :] for v in (shift, scale, gate))
        sh_c, sc_c, gt_c = (jnp.broadcast_to(v[b][None, None, :], (b, 1, d)) for v in (shift, scale, gate))
        lg, lb = ln_g[i][None, :], ln_b[i][None, :]
        if is_attn:
            a = i // 2
            lam_init = 0.8 - 0.6 * math.exp(-0.3 * i)
            w_in = attn_w_in[a].astype(bf16)
            w_qk, w_vt, w_gate = w_in[:, :2 * d], w_in[:, 2 * d:3 * d].T, w_in[:, 3 * d:]
            w_out = attn_w_out[a].astype(bf16)
            lams = [v[a][None, :] for v in (attn_lq1, attn_lk1, attn_lq2, attn_lk2)]
            subg = attn_subln_g[a][None, :]
            total = s + ctx_len
            sh_2, sc_2 = (jnp.stack([vx, vc], axis=1) for vx, vc in ((sh_x, sh_c), (sc_x, sc_c)))
            qk = _project_qk(x, ctx, sh_2, sc_2, w_qk, tabs)
            vt = _project_vt(x, ctx, sh_2, sc_2, w_vt)
            g_x = _project_f32(x, sh_x, sc_x, w_gate)
            y_x = _attention(qk, vt, g_x, lams, subg, lam_init, (0, s), (0, total))
            if ctx_out:
                g_c = _project_f32(ctx, sh_c, sc_c, w_gate)
                y_c = _attention(qk, vt, g_c, lams, subg, lam_init, (s, ctx_len), (s, ctx_len))
                ctx = _output_layer_norm(y_c, ctx, gt_c, w_out, lg, lb, alpha)
            x = _output_layer_norm(y_x, x, gt_x, w_out, lg, lb, alpha)
        else:
            p = i // 2
            w_in = pool_w_in[p].astype(bf16)
            grp_w = pool_grp_w[p].astype(bf16)
            w_out = pool_w_out[p].astype(bf16)
            cs = pool_scale[p][None, :]
            ug_x = _project_f32(x, sh_x, sc_x, w_in)
            if ctx_out:
                ug_c = _project_f32(ctx, sh_c, sc_c, w_in)
                ctx = _pool_mix_output(ug_c, ctx, gt_c, grp_w, cs, w_out, lg, lb, alpha)
            x = _pool_mix_output(ug_x, x, gt_x, grp_w, cs, w_out, lg, lb, alpha)
    return x
```

```python
import functools
import math

import jax
import jax.numpy as jnp
from jax import lax
from jax.experimental import pallas as pl
from jax.experimental.pallas import tpu as pltpu

N_HEADS = 8
HEAD_DIM = 128
V_DIM = 2 * HEAD_DIM
GRID_W = 64
ROPE_THETA = 10000.0
ROPE_PAIRS = HEAD_DIM // 4
POOL_WINDOWS = (2, 4, 8, 16)
POOL_HALO = 8
LN_EPS = 1e-5
SUBLN_EPS = 1e-5

LANES = 128
SUBLANES = 8
MXU_DEPTH = 256
VMEM_LIMIT_BYTES = 56 * 1024 * 1024

PROJ_ROWS = 512
PROJ_COLS = 2048
ATTN_Q_ROWS = 2048
ATTN_K_ROWS = 1280
MOD_COLS = 1024
MOD_ROWS = SUBLANES

LOG2E = 1.4426950408889634


def _silu(v):
    return v / (1.0 + jnp.exp(-v))


def _params(*semantics):
    return pltpu.CompilerParams(dimension_semantics=semantics, vmem_limit_bytes=VMEM_LIMIT_BYTES)


def _mod_kernel(c_ref, w_ref, b_ref, o_ref):
    s = _silu(c_ref[...])
    o_ref[...] = jnp.dot(s, w_ref[...], precision=lax.Precision.HIGHEST,
                         preferred_element_type=jnp.float32) + b_ref[...]


def _modulation(cvec, mod_w, mod_b):
    depth, d, n = mod_w.shape
    tn = min(MOD_COLS, n)
    return pl.pallas_call(
        _mod_kernel,
        grid=(depth, n // tn),
        in_specs=[
            pl.BlockSpec((MOD_ROWS, d), lambda l, j: (0, 0)),
            pl.BlockSpec((None, d, tn), lambda l, j: (l, 0, j)),
            pl.BlockSpec((None, 1, tn), lambda l, j: (l, 0, j)),
        ],
        out_specs=pl.BlockSpec((None, MOD_ROWS, tn), lambda l, j: (l, 0, j)),
        out_shape=jax.ShapeDtypeStruct((depth, MOD_ROWS, n), jnp.float32),
        compiler_params=_params("parallel", "parallel"),
        name="modulation",
    )(cvec, mod_w, mod_b.reshape(depth, 1, n))


def _modulated_matmul(x_ref, sh_ref, sc_ref, w_ref):
    h = (x_ref[...] * (1.0 + sc_ref[...]) + sh_ref[...]).astype(jnp.bfloat16)
    return jnp.dot(h, w_ref[...], preferred_element_type=jnp.float32)


def _proj_f32_kernel(x_ref, sh_ref, sc_ref, w_ref, o_ref):
    o_ref[...] = _modulated_matmul(x_ref, sh_ref, sc_ref, w_ref)


def _modulated_tokens(x_ref, ctx_ref, sh_ref, sc_ref, step, n_latent_steps):
    tokens = jnp.where(step == n_latent_steps, ctx_ref[...], x_ref[...])
    return (tokens * (1.0 + sc_ref[...]) + sh_ref[...]).astype(jnp.bfloat16)


def _token_specs(tm, d, n_latent_steps, grid_rank):
    def lift(fn):
        return (lambda n, bb, i: fn(bb, i)) if grid_rank == 3 else fn
    return [
        pl.BlockSpec((None, tm, d), lift(lambda bb, i: (bb, jnp.minimum(i, n_latent_steps - 1), 0))),
        pl.BlockSpec((None, tm, d), lift(lambda bb, i: (bb, 0, 0))),
        pl.BlockSpec((None, None, 1, d), lift(lambda bb, i: (bb, i // n_latent_steps, 0, 0))),
        pl.BlockSpec((None, None, 1, d), lift(lambda bb, i: (bb, i // n_latent_steps, 0, 0))),
    ]


def _proj_qk_kernel(x_ref, ctx_ref, sh_ref, sc_ref, w_ref, c_ref, sa_ref, sb_ref, o_ref):
    h = _modulated_tokens(x_ref, ctx_ref, sh_ref, sc_ref, pl.program_id(2), pl.num_programs(2) - 1)
    acc = jnp.dot(h, w_ref[...], preferred_element_type=jnp.float32)
    c, sa, sb = c_ref[...], sa_ref[...], sb_ref[...]
    for gi in range(acc.shape[1] // LANES):
        t = acc[:, gi * LANES:(gi + 1) * LANES]
        r = t * c + pltpu.roll(t, LANES - ROPE_PAIRS, 1) * sa + pltpu.roll(t, ROPE_PAIRS, 1) * sb
        head, comp = divmod(gi, V_DIM // LANES)
        o_ref[head, :, comp * LANES:(comp + 1) * LANES] = r.astype(o_ref.dtype)


def _project_qk(x, ctx, shift, scale, w_qk, tables):
    b, s, d = x.shape
    tn = PROJ_COLS
    nt = w_qk.shape[1] // tn
    nh = tn // V_DIM
    tm = PROJ_ROWS
    nx = s // tm
    tab_spec = pl.BlockSpec((None, tm, LANES), lambda n, bb, i: (n, i, 0))
    return pl.pallas_call(
        _proj_qk_kernel,
        grid=(nt, b, nx + 1),
        in_specs=_token_specs(tm, d, nx, 3) + [
            pl.BlockSpec((d, tn), lambda n, bb, i: (0, n)),
            tab_spec, tab_spec, tab_spec,
        ],
        out_specs=pl.BlockSpec((None, None, nh, tm, V_DIM), lambda n, bb, i: (n, bb, 0, i, 0)),
        out_shape=jax.ShapeDtypeStruct((nt, b, nh, (nx + 1) * tm, V_DIM), jnp.bfloat16),
        compiler_params=_params("arbitrary", "arbitrary", "arbitrary"),
        name="project_qk",
    )(x, ctx, shift, scale, w_qk, *tables)


def _proj_vt_kernel(x_ref, ctx_ref, sh_ref, sc_ref, wt_ref, o_ref):
    h = _modulated_tokens(x_ref, ctx_ref, sh_ref, sc_ref, pl.program_id(1), pl.num_programs(1) - 1)
    vt = lax.dot_general(wt_ref[...], h, (((1,), (1,)), ((), ())),
                         preferred_element_type=jnp.float32)
    for head in range(o_ref.shape[0]):
        for t in range(o_ref.shape[1]):
            o_ref[head, t] = vt[head * V_DIM:(head + 1) * V_DIM,
                                t * MXU_DEPTH:(t + 1) * MXU_DEPTH].astype(o_ref.dtype)


def _project_vt(x, ctx, shift, scale, w_vt):
    b, s, d = x.shape
    nh = w_vt.shape[0] // V_DIM
    tm = PROJ_ROWS
    nx = s // tm
    tiles = tm // MXU_DEPTH
    return pl.pallas_call(
        _proj_vt_kernel,
        grid=(b, nx + 1),
        in_specs=_token_specs(tm, d, nx, 2) + [pl.BlockSpec(w_vt.shape, lambda bb, i: (0, 0))],
        out_specs=pl.BlockSpec((None, nh, tiles, V_DIM, MXU_DEPTH), lambda bb, i: (bb, 0, i, 0, 0)),
        out_shape=jax.ShapeDtypeStruct((b, nh, (nx + 1) * tiles, V_DIM, MXU_DEPTH), jnp.bfloat16),
        compiler_params=_params("parallel", "arbitrary"),
        name="project_vt",
    )(x, ctx, shift, scale, w_vt)


def _project_f32(x, shift, scale, w):
    b, s, d = x.shape
    tn = PROJ_COLS
    nt = w.shape[1] // tn
    tm = min(PROJ_ROWS, s)
    return pl.pallas_call(
        _proj_f32_kernel,
        grid=(nt, b, s // tm),
        in_specs=[
            pl.BlockSpec((None, tm, d), lambda n, bb, i: (bb, i, 0)),
            pl.BlockSpec((None, 1, d), lambda n, bb, i: (bb, 0, 0)),
            pl.BlockSpec((None, 1, d), lambda n, bb, i: (bb, 0, 0)),
            pl.BlockSpec((d, tn), lambda n, bb, i: (0, n)),
        ],
        out_specs=pl.BlockSpec((None, tm, tn), lambda n, bb, i: (bb, i, n)),
        out_shape=jax.ShapeDtypeStruct((b, s, nt * tn), jnp.float32),
        compiler_params=_params("arbitrary", "arbitrary", "arbitrary"),
        name="project_f32",
    )(x, shift, scale, w)


def _key_block(skv):
    best = MXU_DEPTH
    for cand in range(MXU_DEPTH, min(ATTN_K_ROWS, skv) + 1, MXU_DEPTH):
        if skv % cand == 0:
            best = cand
    return best


def _attn_kernel(lq1_ref, lk1_ref, lq2_ref, lk2_ref, subg_ref, qc_ref, qn_ref, k_ref, vt_ref, g_ref, o_ref,
                 s_sc, mx_sc, m_sc, l_sc, acc_sc, *, lam_init, tk, nk, kv_start):
    i = pl.program_id(2)
    tq = qc_ref.shape[0]
    n_query_chunks = tq // MXU_DEPTH
    n_key_tiles = tk // MXU_DEPTH

    def scores_piece(jb, next_query_block, comp, qc):
        cols = slice(comp * HEAD_DIM, (comp + 1) * HEAD_DIM)
        qs = slice(qc * MXU_DEPTH, (qc + 1) * MXU_DEPTH)
        q = qc_ref[qs, cols]
        if next_query_block is not None:
            q = jnp.where(next_query_block, qn_ref[qs, cols], q)
        rows = pl.ds(pl.multiple_of(kv_start + jb * tk, MXU_DEPTH), tk)
        st = lax.dot_general(k_ref[rows, cols], q, (((1,), (1,)), ((), ())),
                             preferred_element_type=jnp.float32)
        s_sc[comp, :, qs] = st
        mx_sc[comp, :, qs] = jnp.max(st.reshape(tk // SUBLANES, SUBLANES, MXU_DEPTH), axis=0)

    def block_stats(comp):
        m_blk = jnp.max(mx_sc[comp], axis=0, keepdims=True)
        m_prev = m_sc[comp]
        m_new = jnp.maximum(m_prev, m_blk)
        alpha = jnp.exp2(m_prev - m_new)
        m_sc[comp] = m_new
        return m_new, alpha

    def consume_piece(vt_blk, comp, qc, m_new, alpha):
        qs = slice(qc * MXU_DEPTH, (qc + 1) * MXU_DEPTH)
        st = s_sc[comp, :, qs].reshape(tk // SUBLANES, SUBLANES, MXU_DEPTH)
        p = jnp.exp2(st - jnp.broadcast_to(m_new[:, qs], (SUBLANES, MXU_DEPTH)))
        a = alpha[:, qs]
        l_sc[comp, :, qs] = a * l_sc[comp, :, qs] + jnp.sum(p, axis=0)
        pv = jnp.dot(vt_blk, p.reshape(tk, MXU_DEPTH).astype(jnp.bfloat16),
                     preferred_element_type=jnp.float32)
        acc_sc[comp, :, qs] = a * acc_sc[comp, :, qs] + pv

    def step(jb, carry):
        last = jb == nk - 1
        jn = jnp.where(last, 0, jb + 1)
        tile0 = (kv_start + jb * tk) // MXU_DEPTH
        vt_blk = jnp.concatenate([vt_ref[tile0 + t] for t in range(n_key_tiles)], axis=1)
        stats = [block_stats(comp) for comp in range(2)]
        order = [(comp, qc) for qc in range(n_query_chunks) for comp in range(2)]
        for n, (comp, qc) in enumerate(order):
            consume_piece(vt_blk, comp, qc, *stats[comp])
            if n > 0:
                scores_piece(jn, last, *order[n - 1])
        scores_piece(jn, last, *order[-1])
        return carry

    @pl.when(i == 0)
    def _():
        for comp in range(2):
            for qc in range(n_query_chunks):
                scores_piece(0, None, comp, qc)

    m_sc[...] = jnp.full(m_sc.shape, -jnp.inf, jnp.float32)
    l_sc[...] = jnp.zeros(l_sc.shape, jnp.float32)
    acc_sc[...] = jnp.zeros(acc_sc.shape, jnp.float32)

    lax.fori_loop(0, nk, step, 0)

    lam = (jnp.exp(jnp.sum(lq1_ref[...] * lk1_ref[...], axis=-1, keepdims=True))
           - jnp.exp(jnp.sum(lq2_ref[...] * lk2_ref[...], axis=-1, keepdims=True)) + lam_init)
    l1 = jnp.sum(l_sc[0], axis=0, keepdims=True)
    l2 = jnp.sum(l_sc[1], axis=0, keepdims=True)
    ot = acc_sc[0] / l1 - lam * (acc_sc[1] / l2)
    ot = ot * lax.rsqrt(jnp.mean(ot * ot, axis=0, keepdims=True) + SUBLN_EPS)
    o = ot.T * subg_ref[...] * (1.0 - lam_init)
    o_ref[...] = (o * _silu(g_ref[...])).astype(o_ref.dtype)


def _attention(qk, vt, gate_path, lams, subln_g, lam_init, q_rows, kv_rows):
    _, b, nh, rows, _ = qk.shape
    (q_start, sq), (kv_start, skv) = q_rows, kv_rows
    tq = min(ATTN_Q_ROWS, sq)
    tk = _key_block(skv)
    assert sq % tq == 0 and q_start % tq == 0 and kv_start % MXU_DEPTH == 0
    nq, nk = sq // tq, skv // tk
    q0 = q_start // tq
    small = pl.BlockSpec((1, HEAD_DIM), lambda bb, h, i: (0, 0))
    resident = pl.Buffered(1)
    return pl.pallas_call(
        functools.partial(_attn_kernel, lam_init=lam_init, tk=tk, nk=nk, kv_start=kv_start),
        grid=(b, nh, nq),
        in_specs=[
            small, small, small, small,
            pl.BlockSpec((1, V_DIM), lambda bb, h, i: (0, 0)),
            pl.BlockSpec((None, None, None, tq, V_DIM), lambda bb, h, i: (0, bb, h, q0 + i, 0)),
            pl.BlockSpec((None, None, None, tq, V_DIM),
                         lambda bb, h, i: (0, bb, h, q0 + jnp.minimum(i + 1, nq - 1), 0)),
            pl.BlockSpec((None, None, None, rows, V_DIM), lambda bb, h, i: (1, bb, h, 0, 0),
                         pipeline_mode=resident),
            pl.BlockSpec((None, None, rows // MXU_DEPTH, V_DIM, MXU_DEPTH), lambda bb, h, i: (bb, h, 0, 0, 0),
                         pipeline_mode=resident),
            pl.BlockSpec((None, tq, V_DIM), lambda bb, h, i: (bb, i, h)),
        ],
        out_specs=pl.BlockSpec((None, tq, V_DIM), lambda bb, h, i: (bb, i, h)),
        out_shape=jax.ShapeDtypeStruct((b, sq, nh * V_DIM), jnp.bfloat16),
        scratch_shapes=[
            pltpu.VMEM((2, tk, tq), jnp.float32),
            pltpu.VMEM((2, SUBLANES, tq), jnp.float32),
            pltpu.VMEM((2, 1, tq), jnp.float32),
            pltpu.VMEM((2, SUBLANES, tq), jnp.float32),
            pltpu.VMEM((2, V_DIM, tq), jnp.float32),
        ],
        compiler_params=_params("parallel", "parallel", "arbitrary"),
        name="diff_attention",
    )(*lams, subln_g, qk, qk, qk, vt, gate_path)


def _residual_layer_norm(x, gate, y, ln_g, ln_b, alpha):
    z = alpha * x + gate * y
    mu = jnp.mean(z, axis=-1, keepdims=True)
    zc = z - mu
    var = jnp.mean(zc * zc, axis=-1, keepdims=True)
    return zc * lax.rsqrt(var + LN_EPS) * ln_g + ln_b


def _out_kernel(y_ref, x_ref, gate_ref, w_ref, lg_ref, lb_ref, o_ref, *, alpha):
    y = jnp.dot(y_ref[...], w_ref[...], preferred_element_type=jnp.float32)
    o_ref[...] = _residual_layer_norm(x_ref[...], gate_ref[...], y, lg_ref[...], lb_ref[...], alpha)


def _output_layer_norm(y, x, gate, w_out, ln_g, ln_b, alpha):
    b, s, d = x.shape
    tm = min(PROJ_ROWS, s)
    vec = pl.BlockSpec((1, d), lambda bb, i: (0, 0))
    return pl.pallas_call(
        functools.partial(_out_kernel, alpha=alpha),
        grid=(b, s // tm),
        in_specs=[
            pl.BlockSpec((None, tm, y.shape[2]), lambda bb, i: (bb, i, 0)),
            pl.BlockSpec((None, tm, d), lambda bb, i: (bb, i, 0)),
            pl.BlockSpec((None, 1, d), lambda bb, i: (bb, 0, 0)),
            pl.BlockSpec(w_out.shape, lambda bb, i: (0, 0)),
            vec, vec,
        ],
        out_specs=pl.BlockSpec((None, tm, d), lambda bb, i: (bb, i, 0)),
        out_shape=jax.ShapeDtypeStruct((b, s, d), jnp.float32),
        compiler_params=_params("parallel", "parallel"),
        name="output_layer_norm",
    )(y, x, gate, w_out, ln_g, ln_b)


def _pool_kernel(u_ref, up_ref, un_ref, g_ref, x_ref, gate_ref, gw_ref, cs_ref, w_ref, lg_ref, lb_ref,
                 o_ref, ubuf, mix, *, seq, alpha):
    i = pl.program_id(1)
    tm = u_ref.shape[0]
    gdim = u_ref.shape[1] // len(POOL_WINDOWS)
    ubuf[0:POOL_HALO, :] = jnp.where(i > 0, up_ref[...], 0.0)
    ubuf[POOL_HALO:POOL_HALO + tm, :] = u_ref[...]
    ubuf[POOL_HALO + tm:2 * POOL_HALO + tm, :] = jnp.where(i < pl.num_programs(1) - 1, un_ref[...], 0.0)
    t = i * tm + lax.broadcasted_iota(jnp.int32, (tm, 1), 0)
    for gi, w in enumerate(POOL_WINDOWS):
        lo = w // 2
        hi = w - 1 - lo
        cols = slice(gi * gdim, (gi + 1) * gdim)
        win = ubuf[POOL_HALO - lo:POOL_HALO - lo + tm, cols]
        for off in range(-lo + 1, hi + 1):
            win = win + ubuf[POOL_HALO + off:POOL_HALO + off + tm, cols]
        cnt = (jnp.minimum(t + hi + 1, seq) - jnp.maximum(t - lo, 0)).astype(jnp.float32)
        pooled = win / cnt - u_ref[:, cols]
        mix[:, cols] = jnp.dot(pooled.astype(jnp.bfloat16), gw_ref[gi], preferred_element_type=jnp.float32)
    gated = ((mix[...] * cs_ref[...]) * _silu(g_ref[...])).astype(jnp.bfloat16)
    y = jnp.dot(gated, w_ref[...], preferred_element_type=jnp.float32)
    o_ref[...] = _residual_layer_norm(x_ref[...], gate_ref[...], y, lg_ref[...], lb_ref[...], alpha)


def _pool_mix_output(ug, x, gate, grp_w, ch_scale, w_out, ln_g, ln_b, alpha):
    b, s, d = x.shape
    width = ug.shape[2] // 2
    tm = min(PROJ_ROWS, s)
    halo_blocks = tm // POOL_HALO
    last_halo = s // POOL_HALO - 1
    vec = pl.BlockSpec((1, d), lambda bb, i: (0, 0))
    return pl.pallas_call(
        functools.partial(_pool_kernel, seq=s, alpha=alpha),
        grid=(b, s // tm),
        in_specs=[
            pl.BlockSpec((None, tm, width), lambda bb, i: (bb, i, 0)),
            pl.BlockSpec((None, POOL_HALO, width), lambda bb, i: (bb, jnp.maximum(i * halo_blocks - 1, 0), 0)),
            pl.BlockSpec((None, POOL_HALO, width),
                         lambda bb, i: (bb, jnp.minimum((i + 1) * halo_blocks, last_halo), 0)),
            pl.BlockSpec((None, tm, width), lambda bb, i: (bb, i, 1)),
            pl.BlockSpec((None, tm, d), lambda bb, i: (bb, i, 0)),
            pl.BlockSpec((None, 1, d), lambda bb, i: (bb, 0, 0)),
            pl.BlockSpec(grp_w.shape, lambda bb, i: (0, 0, 0)),
            pl.BlockSpec((1, width), lambda bb, i: (0, 0)),
            pl.BlockSpec(w_out.shape, lambda bb, i: (0, 0)),
            vec, vec,
        ],
        out_specs=pl.BlockSpec((None, tm, d), lambda bb, i: (bb, i, 0)),
        out_shape=jax.ShapeDtypeStruct((b, s, d), jnp.float32),
        scratch_shapes=[
            pltpu.VMEM((tm + 2 * POOL_HALO, width), jnp.float32),
            pltpu.VMEM((tm, width), jnp.float32),
        ],
        compiler_params=_params("parallel", "parallel"),
        name="pool_mix_output",
    )(ug, ug, ug, ug, x, gate, grp_w, ch_scale, w_out, ln_g, ln_b)


def _rope_tables(s, total_rows, q_scale):
    t = jnp.arange(s)
    t_row = (t // GRID_W).astype(jnp.float32)
    t_col = (t % GRID_W).astype(jnp.float32)
    inv_freq = ROPE_THETA ** (-jnp.arange(ROPE_PAIRS, dtype=jnp.float32) / ROPE_PAIRS)
    ang_r = t_row[:, None] * inv_freq
    ang_c = t_col[:, None] * inv_freq
    cr, sr, cc, sc = jnp.cos(ang_r), jnp.sin(ang_r), jnp.cos(ang_c), jnp.sin(ang_c)
    z = jnp.zeros_like(sr)
    c = jnp.concatenate([cr, cr, cc, cc], axis=-1)
    sa = jnp.concatenate([-sr, z, -sc, z], axis=-1)
    sb = jnp.concatenate([z, sr, z, sc], axis=-1)
    rest = total_rows - s
    c = jnp.concatenate([c, jnp.ones((rest, LANES), jnp.float32)], axis=0)
    sa, sb = (jnp.concatenate([tab, jnp.zeros((rest, LANES), jnp.float32)], axis=0) for tab in (sa, sb))
    return tuple(jnp.stack([tab * q_scale, tab]) for tab in (c, sa, sb))


def kernel(x, c, ctx, c_ctx, mod_w, mod_b, ln_g, ln_b, attn_w_in, attn_w_out, attn_lq1, attn_lk1, attn_lq2,
           attn_lk2, attn_subln_g, pool_w_in, pool_grp_w, pool_scale, pool_w_out):
    b, s, d = x.shape
    ctx_len = ctx.shape[1]
    depth = mod_w.shape[0]
    assert d == N_HEADS * V_DIM and b + 1 <= MOD_ROWS
    assert s % PROJ_ROWS == 0 and ctx_len <= PROJ_ROWS
    alpha = (2 * depth) ** 0.25
    q_scale = LOG2E / math.sqrt(HEAD_DIM)
    bf16 = jnp.bfloat16

    cvec = jnp.concatenate([c, c_ctx[None, :], jnp.zeros((MOD_ROWS - b - 1, d), jnp.float32)], axis=0)
    mod = _modulation(cvec, mod_w, mod_b)
    tabs = _rope_tables(s, s + PROJ_ROWS, q_scale)

    for i in range(depth):
        is_attn = i % 2 == 0
        ctx_out = any(j % 2 == 0 for j in range(i + 1, depth))
        shift, scale, gate = (mod[i, :, k * d:(k + 1) * d] for k in range(3))
        sh_x, sc_x, gt_x = (v[:b, None, :] for v in (shift, scale, gate))
        sh_c, sc_c, gt_c = (jnp.broadcast_to(v[b][None, None, :], (b, 1, d)) for v in (shift, scale, gate))
        lg, lb = ln_g[i][None, :], ln_b[i][None, :]
        if is_attn:
            a = i // 2
            lam_init = 0.8 - 0.6 * math.exp(-0.3 * i)
            w_in = attn_w_in[a].astype(bf16)
            w_qk, w_vt, w_gate = w_in[:, :2 * d], w_in[:, 2 * d:3 * d].T, w_in[:, 3 * d:]
            w_out = attn_w_out[a].astype(bf16)
            lams = [v[a][None, :] for v in (attn_lq1, attn_lk1, attn_lq2, attn_lk2)]
            subg = attn_subln_g[a][None, :]
            total = s + ctx_len
            sh_2, sc_2 = (jnp.stack([vx, vc], axis=1) for vx, vc in ((sh_x, sh_c), (sc_x, sc_c)))
            ctx_pad = jnp.pad(ctx, ((0, 0), (0, PROJ_ROWS - ctx_len), (0, 0)))
            qk = _project_qk(x, ctx_pad, sh_2, sc_2, w_qk, tabs)
            vt = _project_vt(x, ctx_pad, sh_2, sc_2, w_vt)
            g_x = _project_f32(x, sh_x, sc_x, w_gate)
            y_x = _attention(qk, vt, g_x, lams, subg, lam_init, (0, s), (0, total))
            if ctx_out:
                g_c = _project_f32(ctx, sh_c, sc_c, w_gate)
                y_c = _attention(qk, vt, g_c, lams, subg, lam_init, (s, ctx_len), (s, ctx_len))
                ctx = _output_layer_norm(y_c, ctx, gt_c, w_out, lg, lb, alpha)
            x = _output_layer_norm(y_x, x, gt_x, w_out, lg, lb, alpha)
        else:
            p = i // 2
            w_in = pool_w_in[p].astype(bf16)
            grp_w = pool_grp_w[p].astype(bf16)
            w_out = pool_w_out[p].astype(bf16)
            cs = pool_scale[p][None, :]
            ug_x = _project_f32(x, sh_x, sc_x, w_in)
            if ctx_out:
                ug_c = _project_f32(ctx, sh_c, sc_c, w_in)
                ctx = _pool_mix_output(ug_c, ctx, gt_c, grp_w, cs, w_out, lg, lb, alpha)
            x = _pool_mix_output(ug_x, x, gt_x, grp_w, cs, w_out, lg, lb, alpha)
    return x
```

```python
import functools
import math

import jax
import jax.numpy as jnp
from jax import lax
from jax.experimental import pallas as pl
from jax.experimental.pallas import tpu as pltpu

N_HEADS = 8
HEAD_DIM = 128
V_DIM = 2 * HEAD_DIM
GRID_W = 64
ROPE_THETA = 10000.0
ROPE_PAIRS = HEAD_DIM // 4
POOL_WINDOWS = (2, 4, 8, 16)
POOL_HALO = 8
LN_EPS = 1e-5
SUBLN_EPS = 1e-5

LANES = 128
SUBLANES = 8
MXU_DEPTH = 256
VMEM_LIMIT_BYTES = 56 * 1024 * 1024

PROJ_ROWS = 512
PROJ_COLS = 2048
ATTN_Q_ROWS = 2048
ATTN_K_ROWS = 1280
MOD_COLS = 1024
MOD_ROWS = SUBLANES

LOG2E = 1.4426950408889634


def _silu(v):
    return v / (1.0 + jnp.exp(-v))


def _params(*semantics):
    return pltpu.CompilerParams(dimension_semantics=semantics, vmem_limit_bytes=VMEM_LIMIT_BYTES)


def _mod_kernel(c_ref, w_ref, b_ref, o_ref):
    s = _silu(c_ref[...])
    o_ref[...] = jnp.dot(s, w_ref[...], precision=lax.Precision.HIGHEST,
                         preferred_element_type=jnp.float32) + b_ref[...]


def _modulation(cvec, mod_w, mod_b):
    depth, d, n = mod_w.shape
    tn = min(MOD_COLS, n)
    return pl.pallas_call(
        _mod_kernel,
        grid=(depth, n // tn),
        in_specs=[
            pl.BlockSpec((MOD_ROWS, d), lambda l, j: (0, 0)),
            pl.BlockSpec((None, d, tn), lambda l, j: (l, 0, j)),
            pl.BlockSpec((None, 1, tn), lambda l, j: (l, 0, j)),
        ],
        out_specs=pl.BlockSpec((None, MOD_ROWS, tn), lambda l, j: (l, 0, j)),
        out_shape=jax.ShapeDtypeStruct((depth, MOD_ROWS, n), jnp.float32),
        compiler_params=_params("parallel", "parallel"),
        name="modulation",
    )(cvec, mod_w, mod_b.reshape(depth, 1, n))


def _modulated_matmul(x_ref, sh_ref, sc_ref, w_ref):
    h = (x_ref[...] * (1.0 + sc_ref[...]) + sh_ref[...]).astype(jnp.bfloat16)
    return jnp.dot(h, w_ref[...], preferred_element_type=jnp.float32)


def _proj_f32_kernel(x_ref, sh_ref, sc_ref, w_ref, o_ref):
    o_ref[...] = _modulated_matmul(x_ref, sh_ref, sc_ref, w_ref)


def _modulated_tokens(x_ref, ctx_ref, sh_ref, sc_ref, step, n_latent_steps):
    tokens = jnp.where(step == n_latent_steps, ctx_ref[...], x_ref[...])
    return (tokens * (1.0 + sc_ref[...]) + sh_ref[...]).astype(jnp.bfloat16)


def _token_specs(tm, d, n_latent_steps, grid_rank):
    def lift(fn):
        return (lambda n, bb, i: fn(bb, i)) if grid_rank == 3 else fn
    return [
        pl.BlockSpec((None, tm, d), lift(lambda bb, i: (bb, jnp.minimum(i, n_latent_steps - 1), 0))),
        pl.BlockSpec((None, tm, d), lift(lambda bb, i: (bb, 0, 0))),
        pl.BlockSpec((None, None, 1, d), lift(lambda bb, i: (bb, i // n_latent_steps, 0, 0))),
        pl.BlockSpec((None, None, 1, d), lift(lambda bb, i: (bb, i // n_latent_steps, 0, 0))),
    ]


def _proj_qk_kernel(x_ref, ctx_ref, sh_ref, sc_ref, w_ref, c_ref, sa_ref, sb_ref, o_ref):
    h = _modulated_tokens(x_ref, ctx_ref, sh_ref, sc_ref, pl.program_id(2), pl.num_programs(2) - 1)
    acc = jnp.dot(h, w_ref[...], preferred_element_type=jnp.float32)
    c, sa, sb = c_ref[...], sa_ref[...], sb_ref[...]
    for gi in range(acc.shape[1] // LANES):
        t = acc[:, gi * LANES:(gi + 1) * LANES]
        r = t * c + pltpu.roll(t, LANES - ROPE_PAIRS, 1) * sa + pltpu.roll(t, ROPE_PAIRS, 1) * sb
        head, comp = divmod(gi, V_DIM // LANES)
        o_ref[head, :, comp * LANES:(comp + 1) * LANES] = r.astype(o_ref.dtype)


def _project_qk(x, ctx, shift, scale, w_qk, tables):
    b, s, d = x.shape
    tn = PROJ_COLS
    nt = w_qk.shape[1] // tn
    nh = tn // V_DIM
    tm = PROJ_ROWS
    nx = s // tm
    tab_spec = pl.BlockSpec((None, tm, LANES), lambda n, bb, i: (n, i, 0))
    return pl.pallas_call(
        _proj_qk_kernel,
        grid=(nt, b, nx + 1),
        in_specs=_token_specs(tm, d, nx, 3) + [
            pl.BlockSpec((d, tn), lambda n, bb, i: (0, n)),
            tab_spec, tab_spec, tab_spec,
        ],
        out_specs=pl.BlockSpec((None, None, nh, tm, V_DIM), lambda n, bb, i: (n, bb, 0, i, 0)),
        out_shape=jax.ShapeDtypeStruct((nt, b, nh, (nx + 1) * tm, V_DIM), jnp.bfloat16),
        compiler_params=_params("arbitrary", "arbitrary", "arbitrary"),
        name="project_qk",
    )(x, ctx, shift, scale, w_qk, *tables)


def _proj_vt_kernel(x_ref, ctx_ref, sh_ref, sc_ref, wt_ref, o_ref):
    h = _modulated_tokens(x_ref, ctx_ref, sh_ref, sc_ref, pl.program_id(1), pl.num_programs(1) - 1)
    vt = lax.dot_general(wt_ref[...], h, (((1,), (1,)), ((), ())),
                         preferred_element_type=jnp.float32)
    for head in range(o_ref.shape[0]):
        o_ref[head] = vt[head * V_DIM:(head + 1) * V_DIM, :].astype(o_ref.dtype)


def _project_vt(x, ctx, shift, scale, w_vt):
    b, s, d = x.shape
    nh = w_vt.shape[0] // V_DIM
    tm = PROJ_ROWS
    nx = s // tm
    return pl.pallas_call(
        _proj_vt_kernel,
        grid=(b, nx + 1),
        in_specs=_token_specs(tm, d, nx, 2) + [pl.BlockSpec(w_vt.shape, lambda bb, i: (0, 0))],
        out_specs=pl.BlockSpec((None, nh, V_DIM, tm), lambda bb, i: (bb, 0, 0, i)),
        out_shape=jax.ShapeDtypeStruct((b, nh, V_DIM, (nx + 1) * tm), jnp.bfloat16),
        compiler_params=_params("parallel", "arbitrary"),
        name="project_vt",
    )(x, ctx, shift, scale, w_vt)


def _project_f32(x, shift, scale, w):
    b, s, d = x.shape
    tn = PROJ_COLS
    nt = w.shape[1] // tn
    tm = min(PROJ_ROWS, s)
    return pl.pallas_call(
        _proj_f32_kernel,
        grid=(nt, b, s // tm),
        in_specs=[
            pl.BlockSpec((None, tm, d), lambda n, bb, i: (bb, i, 0)),
            pl.BlockSpec((None, 1, d), lambda n, bb, i: (bb, 0, 0)),
            pl.BlockSpec((None, 1, d), lambda n, bb, i: (bb, 0, 0)),
            pl.BlockSpec((d, tn), lambda n, bb, i: (0, n)),
        ],
        out_specs=pl.BlockSpec((None, tm, tn), lambda n, bb, i: (bb, i, n)),
        out_shape=jax.ShapeDtypeStruct((b, s, nt * tn), jnp.float32),
        compiler_params=_params("arbitrary", "arbitrary", "arbitrary"),
        name="project_f32",
    )(x, shift, scale, w)


def _key_block(skv):
    best = MXU_DEPTH
    for cand in range(MXU_DEPTH, min(ATTN_K_ROWS, skv) + 1, MXU_DEPTH):
        if skv % cand == 0:
            best = cand
    return best


def _attn_kernel(par_ref, q_ref, k0_ref, kn_ref, vt_ref, g_ref, o_ref,
                 s_sc, mx_sc, m_sc, l_sc, acc_sc, *, lam_init):
    i = pl.program_id(2)
    j = pl.program_id(3)
    tk = kn_ref.shape[0]
    tq = q_ref.shape[0]
    n_query_chunks = tq // MXU_DEPTH

    def scores_piece(kb_ref, comp, qc):
        cols = slice(comp * HEAD_DIM, (comp + 1) * HEAD_DIM)
        qs = slice(qc * MXU_DEPTH, (qc + 1) * MXU_DEPTH)
        st = lax.dot_general(kb_ref[:, cols], q_ref[qs, cols], (((1,), (1,)), ((), ())),
                             preferred_element_type=jnp.float32)
        s_sc[comp, :, qs] = st
        mx_sc[comp, :, qs] = jnp.max(st.reshape(tk // SUBLANES, SUBLANES, MXU_DEPTH), axis=0)

    def block_stats(comp):
        m_blk = jnp.max(mx_sc[comp], axis=0, keepdims=True)
        m_prev = m_sc[comp]
        m_new = jnp.maximum(m_prev, m_blk)
        alpha = jnp.exp2(m_prev - m_new)
        m_sc[comp] = m_new
        return m_new, alpha

    def consume_piece(comp, qc, m_new, alpha):
        qs = slice(qc * MXU_DEPTH, (qc + 1) * MXU_DEPTH)
        st = s_sc[comp, :, qs].reshape(tk // SUBLANES, SUBLANES, MXU_DEPTH)
        p = jnp.exp2(st - jnp.broadcast_to(m_new[:, qs], (SUBLANES, MXU_DEPTH)))
        a = alpha[:, qs]
        l_sc[comp, :, qs] = a * l_sc[comp, :, qs] + jnp.sum(p, axis=0)
        pv = jnp.dot(vt_ref[...], p.reshape(tk, MXU_DEPTH).astype(jnp.bfloat16),
                     preferred_element_type=jnp.float32)
        acc_sc[comp, :, qs] = a * acc_sc[comp, :, qs] + pv

    def step():
        stats = [block_stats(comp) for comp in range(2)]
        order = [(comp, qc) for qc in range(n_query_chunks) for comp in range(2)]
        for n, (comp, qc) in enumerate(order):
            consume_piece(comp, qc, *stats[comp])
            if n > 0:
                scores_piece(kn_ref, *order[n - 1])
        scores_piece(kn_ref, *order[-1])

    @pl.when((i == 0) & (j == 0))
    def _():
        for comp in range(2):
            for qc in range(n_query_chunks):
                scores_piece(k0_ref, comp, qc)

    @pl.when(j == 0)
    def _():
        m_sc[...] = jnp.full(m_sc.shape, -jnp.inf, jnp.float32)
        l_sc[...] = jnp.zeros(l_sc.shape, jnp.float32)
        acc_sc[...] = jnp.zeros(acc_sc.shape, jnp.float32)

    step()

    @pl.when(j == pl.num_programs(3) - 1)
    def _():
        lq1, lk1, lq2, lk2 = (par_ref[r:r + 1, 0:HEAD_DIM] for r in range(4))
        lam = (jnp.exp(jnp.sum(lq1 * lk1, axis=-1, keepdims=True))
               - jnp.exp(jnp.sum(lq2 * lk2, axis=-1, keepdims=True)) + lam_init)
        l1 = jnp.sum(l_sc[0], axis=0, keepdims=True)
        l2 = jnp.sum(l_sc[1], axis=0, keepdims=True)
        ot = acc_sc[0] / l1 - lam * (acc_sc[1] / l2)
        ot = ot * lax.rsqrt(jnp.mean(ot * ot, axis=0, keepdims=True) + SUBLN_EPS)
        o = ot.T * par_ref[4:5, :] * (1.0 - lam_init)
        o_ref[...] = (o * _silu(g_ref[...])).astype(o_ref.dtype)


def _attention(qk, vt, gate_path, lams, subln_g, lam_init, q_rows, kv_rows):
    _, b, nh, _, _ = qk.shape
    (q_start, sq), (kv_start, skv) = q_rows, kv_rows
    tq = min(ATTN_Q_ROWS, sq)
    tk = _key_block(skv)
    assert sq % tq == 0 and q_start % tq == 0 and kv_start % tk == 0
    nq, nk = sq // tq, skv // tk
    q0, k0 = q_start // tq, kv_start // tk
    pad = jnp.zeros((1, V_DIM - HEAD_DIM), jnp.float32)
    params = jnp.concatenate([jnp.concatenate([v, pad], axis=1) for v in lams] + [subln_g]
                             + [jnp.zeros((SUBLANES - 5, V_DIM), jnp.float32)], axis=0)
    return pl.pallas_call(
        functools.partial(_attn_kernel, lam_init=lam_init),
        grid=(b, nh, nq, nk),
        in_specs=[
            pl.BlockSpec((SUBLANES, V_DIM), lambda bb, h, i, j: (0, 0)),
            pl.BlockSpec((None, None, None, tq, V_DIM),
                         lambda bb, h, i, j: (0, bb, h, q0 + jnp.minimum(i + (j + 1) // nk, nq - 1), 0)),
            pl.BlockSpec((None, None, None, tk, V_DIM), lambda bb, h, i, j: (1, bb, h, k0, 0)),
            pl.BlockSpec((None, None, None, tk, V_DIM), lambda bb, h, i, j: (1, bb, h, k0 + (j + 1) % nk, 0)),
            pl.BlockSpec((None, None, V_DIM, tk), lambda bb, h, i, j: (bb, h, 0, k0 + j)),
            pl.BlockSpec((None, tq, V_DIM), lambda bb, h, i, j: (bb, i, h)),
        ],
        out_specs=pl.BlockSpec((None, tq, V_DIM), lambda bb, h, i, j: (bb, i, h)),
        out_shape=jax.ShapeDtypeStruct((b, sq, nh * V_DIM), jnp.bfloat16),
        scratch_shapes=[
            pltpu.VMEM((2, tk, tq), jnp.float32),
            pltpu.VMEM((2, SUBLANES, tq), jnp.float32),
            pltpu.VMEM((2, 1, tq), jnp.float32),
            pltpu.VMEM((2, SUBLANES, tq), jnp.float32),
            pltpu.VMEM((2, V_DIM, tq), jnp.float32),
        ],
        compiler_params=_params("parallel", "parallel", "arbitrary", "arbitrary"),
        name="diff_attention",
    )(params, qk, qk, qk, vt, gate_path)


def _residual_layer_norm(x, gate, y, ln_g, ln_b, alpha):
    z = alpha * x + gate * y
    mu = jnp.mean(z, axis=-1, keepdims=True)
    zc = z - mu
    var = jnp.mean(zc * zc, axis=-1, keepdims=True)
    return zc * lax.rsqrt(var + LN_EPS) * ln_g + ln_b


def _out_kernel(y_ref, x_ref, gate_ref, w_ref, lg_ref, lb_ref, o_ref, *, alpha):
    y = jnp.dot(y_ref[...], w_ref[...], preferred_element_type=jnp.float32)
    o_ref[...] = _residual_layer_norm(x_ref[...], gate_ref[...], y, lg_ref[...], lb_ref[...], alpha)


def _output_layer_norm(y, x, gate, w_out, ln_g, ln_b, alpha):
    b, s, d = x.shape
    tm = min(PROJ_ROWS, s)
    vec = pl.BlockSpec((1, d), lambda bb, i: (0, 0))
    return pl.pallas_call(
        functools.partial(_out_kernel, alpha=alpha),
        grid=(b, s // tm),
        in_specs=[
            pl.BlockSpec((None, tm, y.shape[2]), lambda bb, i: (bb, i, 0)),
            pl.BlockSpec((None, tm, d), lambda bb, i: (bb, i, 0)),
            pl.BlockSpec((None, 1, d), lambda bb, i: (bb, 0, 0)),
            pl.BlockSpec(w_out.shape, lambda bb, i: (0, 0)),
            vec, vec,
        ],
        out_specs=pl.BlockSpec((None, tm, d), lambda bb, i: (bb, i, 0)),
        out_shape=jax.ShapeDtypeStruct((b, s, d), jnp.float32),
        compiler_params=_params("parallel", "parallel"),
        name="output_layer_norm",
    )(y, x, gate, w_out, ln_g, ln_b)


def _pool_kernel(u_ref, up_ref, un_ref, g_ref, x_ref, gate_ref, gw_ref, cs_ref, w_ref, lg_ref, lb_ref,
                 o_ref, ubuf, mix, *, seq, alpha):
    i = pl.program_id(1)
    tm = u_ref.shape[0]
    gdim = u_ref.shape[1] // len(POOL_WINDOWS)
    ubuf[0:POOL_HALO, :] = jnp.where(i > 0, up_ref[...], 0.0)
    ubuf[POOL_HALO:POOL_HALO + tm, :] = u_ref[...]
    ubuf[POOL_HALO + tm:2 * POOL_HALO + tm, :] = jnp.where(i < pl.num_programs(1) - 1, un_ref[...], 0.0)
    t = i * tm + lax.broadcasted_iota(jnp.int32, (tm, 1), 0)
    rows = tm + 2 * POOL_HALO
    for gi, w in enumerate(POOL_WINDOWS):
        lo = w // 2
        hi = w - 1 - lo
        assert hi == lo - 1 and lo & (lo - 1) == 0 and lo <= POOL_HALO
        cols = slice(gi * gdim, (gi + 1) * gdim)
        fwd = ubuf[:, cols]
        span = 1
        while span < lo:
            fwd = fwd + pltpu.roll(fwd, rows - span, 0)
            span *= 2
        win = (fwd + pltpu.roll(fwd, lo, 0))[POOL_HALO:POOL_HALO + tm]
        cnt = (jnp.minimum(t + hi + 1, seq) - jnp.maximum(t - lo, 0)).astype(jnp.float32)
        pooled = win / cnt - u_ref[:, cols]
        mix[:, cols] = jnp.dot(pooled.astype(jnp.bfloat16), gw_ref[gi], preferred_element_type=jnp.float32)
    gated = ((mix[...] * cs_ref[...]) * _silu(g_ref[...])).astype(jnp.bfloat16)
    y = jnp.dot(gated, w_ref[...], preferred_element_type=jnp.float32)
    o_ref[...] = _residual_layer_norm(x_ref[...], gate_ref[...], y, lg_ref[...], lb_ref[...], alpha)


def _pool_mix_output(ug, x, gate, grp_w, ch_scale, w_out, ln_g, ln_b, alpha):
    b, s, d = x.shape
    width = ug.shape[2] // 2
    tm = min(PROJ_ROWS, s)
    halo_blocks = tm // POOL_HALO
    last_halo = s // POOL_HALO - 1
    vec = pl.BlockSpec((1, d), lambda bb, i: (0, 0))
    return pl.pallas_call(
        functools.partial(_pool_kernel, seq=s, alpha=alpha),
        grid=(b, s // tm),
        in_specs=[
            pl.BlockSpec((None, tm, width), lambda bb, i: (bb, i, 0)),
            pl.BlockSpec((None, POOL_HALO, width), lambda bb, i: (bb, jnp.maximum(i * halo_blocks - 1, 0), 0)),
            pl.BlockSpec((None, POOL_HALO, width),
                         lambda bb, i: (bb, jnp.minimum((i + 1) * halo_blocks, last_halo), 0)),
            pl.BlockSpec((None, tm, width), lambda bb, i: (bb, i, 1)),
            pl.BlockSpec((None, tm, d), lambda bb, i: (bb, i, 0)),
            pl.BlockSpec((None, 1, d), lambda bb, i: (bb, 0, 0)),
            pl.BlockSpec(grp_w.shape, lambda bb, i: (0, 0, 0)),
            pl.BlockSpec((1, width), lambda bb, i: (0, 0)),
            pl.BlockSpec(w_out.shape, lambda bb, i: (0, 0)),
            vec, vec,
        ],
        out_specs=pl.BlockSpec((None, tm, d), lambda bb, i: (bb, i, 0)),
        out_shape=jax.ShapeDtypeStruct((b, s, d), jnp.float32),
        scratch_shapes=[
            pltpu.VMEM((tm + 2 * POOL_HALO, width), jnp.float32),
            pltpu.VMEM((tm, width), jnp.float32),
        ],
        compiler_params=_params("parallel", "parallel"),
        name="pool_mix_output",
    )(ug, ug, ug, ug, x, gate, grp_w, ch_scale, w_out, ln_g, ln_b)


def _rope_tables(s, total_rows, q_scale):
    t = jnp.arange(s)
    t_row = (t // GRID_W).astype(jnp.float32)
    t_col = (t % GRID_W).astype(jnp.float32)
    inv_freq = ROPE_THETA ** (-jnp.arange(ROPE_PAIRS, dtype=jnp.float32) / ROPE_PAIRS)
    ang_r = t_row[:, None] * inv_freq
    ang_c = t_col[:, None] * inv_freq
    cr, sr, cc, sc = jnp.cos(ang_r), jnp.sin(ang_r), jnp.cos(ang_c), jnp.sin(ang_c)
    z = jnp.zeros_like(sr)
    c = jnp.concatenate([cr, cr, cc, cc], axis=-1)
    sa = jnp.concatenate([-sr, z, -sc, z], axis=-1)
    sb = jnp.concatenate([z, sr, z, sc], axis=-1)
    rest = total_rows - s
    c = jnp.concatenate([c, jnp.ones((rest, LANES), jnp.float32)], axis=0)
    sa, sb = (jnp.concatenate([tab, jnp.zeros((rest, LANES), jnp.float32)], axis=0) for tab in (sa, sb))
    return tuple(jnp.stack([tab * q_scale, tab]) for tab in (c, sa, sb))


def kernel(x, c, ctx, c_ctx, mod_w, mod_b, ln_g, ln_b, attn_w_in, attn_w_out, attn_lq1, attn_lk1, attn_lq2,
           attn_lk2, attn_subln_g, pool_w_in, pool_grp_w, pool_scale, pool_w_out):
    b, s, d = x.shape
    ctx_len = ctx.shape[1]
    depth = mod_w.shape[0]
    assert d == N_HEADS * V_DIM and b + 1 <= MOD_ROWS
    assert s % PROJ_ROWS == 0 and ctx_len <= PROJ_ROWS
    alpha = (2 * depth) ** 0.25
    q_scale = LOG2E / math.sqrt(HEAD_DIM)
    bf16 = jnp.bfloat16

    cvec = jnp.concatenate([c, c_ctx[None, :], jnp.zeros((MOD_ROWS - b - 1, d), jnp.float32)], axis=0)
    mod = _modulation(cvec, mod_w, mod_b)
    tabs = _rope_tables(s, s + PROJ_ROWS, q_scale)

    for i in range(depth):
        is_attn = i % 2 == 0
        ctx_out = any(j % 2 == 0 for j in range(i + 1, depth))
        shift, scale, gate = (mod[i, :, k * d:(k + 1) * d] for k in range(3))
        sh_x, sc_x, gt_x = (v[:b, None, :] for v in (shift, scale, gate))
        sh_c, sc_c, gt_c = (jnp.broadcast_to(v[b][None, None, :], (b, 1, d)) for v in (shift, scale, gate))
        lg, lb = ln_g[i][None, :], ln_b[i][None, :]
        if is_attn:
            a = i // 2
            lam_init = 0.8 - 0.6 * math.exp(-0.3 * i)
            w_in = attn_w_in[a].astype(bf16)
            w_qk, w_vt, w_gate = w_in[:, :2 * d], w_in[:, 2 * d:3 * d].T, w_in[:, 3 * d:]
            w_out = attn_w_out[a].astype(bf16)
            lams = [v[a][None, :] for v in (attn_lq1, attn_lk1, attn_lq2, attn_lk2)]
            subg = attn_subln_g[a][None, :]
            total = s + ctx_len
            sh_2, sc_2 = (jnp.stack([vx, vc], axis=1) for vx, vc in ((sh_x, sh_c), (sc_x, sc_c)))
            ctx_pad = jnp.pad(ctx, ((0, 0), (0, PROJ_ROWS - ctx_len), (0, 0)))
            qk = _project_qk(x, ctx_pad, sh_2, sc_2, w_qk, tabs)
            vt = _project_vt(x, ctx_pad, sh_2, sc_2, w_vt)
            g_x = _project_f32(x, sh_x, sc_x, w_gate)
            y_x = _attention(qk, vt, g_x, lams, subg, lam_init, (0, s), (0, total))
            if ctx_out:
                g_c = _project_f32(ctx, sh_c, sc_c, w_gate)
                y_c = _attention(qk, vt, g_c, lams, subg, lam_init, (s, ctx_len), (s, ctx_len))
                ctx = _output_layer_norm(y_c, ctx, gt_c, w_out, lg, lb, alpha)
            x = _output_layer_norm(y_x, x, gt_x, w_out, lg, lb, alpha)
        else:
            p = i // 2
            w_in = pool_w_in[p].astype(bf16)
            grp_w = pool_grp_w[p].astype(bf16)
            w_out = pool_w_out[p].astype(bf16)
            cs = pool_scale[p][None, :]
            ug_x = _project_f32(x, sh_x, sc_x, w_in)
            if ctx_out:
                ug_c = _project_f32(ctx, sh_c, sc_c, w_in)
                ctx = _pool_mix_output(ug_c, ctx, gt_c, grp_w, cs, w_out, lg, lb, alpha)
            x = _pool_mix_output(ug_x, x, gt_x, grp_w, cs, w_out, lg, lb, alpha)
    return x
```

```python
import functools
import math

import jax
import jax.numpy as jnp
from jax import lax
from jax.experimental import pallas as pl
from jax.experimental.pallas import tpu as pltpu

N_HEADS = 8
HEAD_DIM = 128
V_DIM = 2 * HEAD_DIM
GRID_W = 64
ROPE_THETA = 10000.0
ROPE_PAIRS = HEAD_DIM // 4
POOL_WINDOWS = (2, 4, 8, 16)
POOL_HALO = 8
LN_EPS = 1e-5
SUBLN_EPS = 1e-5

LANES = 128
SUBLANES = 8
MXU_DEPTH = 256
VMEM_LIMIT_BYTES = 56 * 1024 * 1024

PROJ_ROWS = 512
PROJ_COLS = 2048
ATTN_Q_ROWS = 2048
ATTN_K_ROWS = 1280
ATTN_KEY_STEP = 3328
MOD_COLS = 1024
MOD_ROWS = SUBLANES

LOG2E = 1.4426950408889634


def _silu(v):
    return v / (1.0 + jnp.exp(-v))


def _params(*semantics):
    return pltpu.CompilerParams(dimension_semantics=semantics, vmem_limit_bytes=VMEM_LIMIT_BYTES)


def _mod_kernel(c_ref, w_ref, b_ref, o_ref):
    s = _silu(c_ref[...])
    o_ref[...] = jnp.dot(s, w_ref[...], precision=lax.Precision.HIGHEST,
                         preferred_element_type=jnp.float32) + b_ref[...]


def _modulation(cvec, mod_w, mod_b):
    depth, d, n = mod_w.shape
    tn = min(MOD_COLS, n)
    return pl.pallas_call(
        _mod_kernel,
        grid=(depth, n // tn),
        in_specs=[
            pl.BlockSpec((MOD_ROWS, d), lambda l, j: (0, 0)),
            pl.BlockSpec((None, d, tn), lambda l, j: (l, 0, j)),
            pl.BlockSpec((None, 1, tn), lambda l, j: (l, 0, j)),
        ],
        out_specs=pl.BlockSpec((None, MOD_ROWS, tn), lambda l, j: (l, 0, j)),
        out_shape=jax.ShapeDtypeStruct((depth, MOD_ROWS, n), jnp.float32),
        compiler_params=_params("parallel", "parallel"),
        name="modulation",
    )(cvec, mod_w, mod_b.reshape(depth, 1, n))


def _modulated_matmul(x_ref, sh_ref, sc_ref, w_ref):
    h = (x_ref[...] * (1.0 + sc_ref[...]) + sh_ref[...]).astype(jnp.bfloat16)
    return jnp.dot(h, w_ref[...], preferred_element_type=jnp.float32)


def _proj_f32_kernel(x_ref, sh_ref, sc_ref, w_ref, o_ref):
    o_ref[...] = _modulated_matmul(x_ref, sh_ref, sc_ref, w_ref)


def _modulated_tokens(x_ref, ctx_ref, sh_ref, sc_ref, step, n_latent_steps):
    tokens = jnp.where(step == n_latent_steps, ctx_ref[...], x_ref[...])
    return (tokens * (1.0 + sc_ref[...]) + sh_ref[...]).astype(jnp.bfloat16)


def _token_specs(tm, d, n_latent_steps, grid_rank):
    def lift(fn):
        return (lambda n, bb, i: fn(bb, i)) if grid_rank == 3 else fn
    return [
        pl.BlockSpec((None, tm, d), lift(lambda bb, i: (bb, jnp.minimum(i, n_latent_steps - 1), 0))),
        pl.BlockSpec((None, tm, d), lift(lambda bb, i: (bb, 0, 0))),
        pl.BlockSpec((None, None, 1, d), lift(lambda bb, i: (bb, i // n_latent_steps, 0, 0))),
        pl.BlockSpec((None, None, 1, d), lift(lambda bb, i: (bb, i // n_latent_steps, 0, 0))),
    ]


def _proj_qk_kernel(x_ref, ctx_ref, sh_ref, sc_ref, w_ref, c_ref, s_ref, o_ref):
    h = _modulated_tokens(x_ref, ctx_ref, sh_ref, sc_ref, pl.program_id(2), pl.num_programs(2) - 1)
    acc = jnp.dot(h, w_ref[...], preferred_element_type=jnp.float32)
    c, s = c_ref[...], s_ref[...]
    for gi in range(acc.shape[1] // LANES):
        t = acc[:, gi * LANES:(gi + 1) * LANES]
        r = t * c + pltpu.roll(t, LANES // 2, 1) * s
        head, comp = divmod(gi, V_DIM // LANES)
        o_ref[head, :, comp * LANES:(comp + 1) * LANES] = r.astype(o_ref.dtype)


def _project_qk(x, ctx, shift, scale, w_qk, tables):
    b, s, d = x.shape
    tn = PROJ_COLS
    nt = w_qk.shape[1] // tn
    nh = tn // V_DIM
    tm = PROJ_ROWS
    nx = s // tm
    tab_spec = pl.BlockSpec((None, tm, LANES), lambda n, bb, i: (n, i, 0))
    return pl.pallas_call(
        _proj_qk_kernel,
        grid=(nt, b, nx + 1),
        in_specs=_token_specs(tm, d, nx, 3) + [
            pl.BlockSpec((d, tn), lambda n, bb, i: (0, n)),
            tab_spec, tab_spec,
        ],
        out_specs=pl.BlockSpec((None, None, nh, tm, V_DIM), lambda n, bb, i: (n, bb, 0, i, 0)),
        out_shape=jax.ShapeDtypeStruct((nt, b, nh, (nx + 1) * tm, V_DIM), jnp.bfloat16),
        compiler_params=_params("arbitrary", "arbitrary", "arbitrary"),
        name="project_qk",
    )(x, ctx, shift, scale, w_qk, *tables)


def _proj_vt_kernel(x_ref, ctx_ref, sh_ref, sc_ref, wt_ref, o_ref):
    h = _modulated_tokens(x_ref, ctx_ref, sh_ref, sc_ref, pl.program_id(1), pl.num_programs(1) - 1)
    vt = lax.dot_general(wt_ref[...], h, (((1,), (1,)), ((), ())),
                         preferred_element_type=jnp.float32)
    for head in range(o_ref.shape[0]):
        o_ref[head] = vt[head * V_DIM:(head + 1) * V_DIM, :].astype(o_ref.dtype)


def _project_vt(x, ctx, shift, scale, w_vt):
    b, s, d = x.shape
    nh = w_vt.shape[0] // V_DIM
    tm = PROJ_ROWS
    nx = s // tm
    return pl.pallas_call(
        _proj_vt_kernel,
        grid=(b, nx + 1),
        in_specs=_token_specs(tm, d, nx, 2) + [pl.BlockSpec(w_vt.shape, lambda bb, i: (0, 0))],
        out_specs=pl.BlockSpec((None, nh, V_DIM, tm), lambda bb, i: (bb, 0, 0, i)),
        out_shape=jax.ShapeDtypeStruct((b, nh, V_DIM, (nx + 1) * tm), jnp.bfloat16),
        compiler_params=_params("parallel", "arbitrary"),
        name="project_vt",
    )(x, ctx, shift, scale, w_vt)


def _project_f32(x, shift, scale, w):
    b, s, d = x.shape
    tn = PROJ_COLS
    nt = w.shape[1] // tn
    tm = min(PROJ_ROWS, s)
    return pl.pallas_call(
        _proj_f32_kernel,
        grid=(nt, b, s // tm),
        in_specs=[
            pl.BlockSpec((None, tm, d), lambda n, bb, i: (bb, i, 0)),
            pl.BlockSpec((None, 1, d), lambda n, bb, i: (bb, 0, 0)),
            pl.BlockSpec((None, 1, d), lambda n, bb, i: (bb, 0, 0)),
            pl.BlockSpec((d, tn), lambda n, bb, i: (0, n)),
        ],
        out_specs=pl.BlockSpec((None, tm, tn), lambda n, bb, i: (bb, i, n)),
        out_shape=jax.ShapeDtypeStruct((b, s, nt * tn), jnp.float32),
        compiler_params=_params("arbitrary", "arbitrary", "arbitrary"),
        name="project_f32",
    )(x, shift, scale, w)


def _key_plan(skv):
    step = MXU_DEPTH
    for cand in range(MXU_DEPTH, min(ATTN_KEY_STEP, skv) + 1, MXU_DEPTH):
        if skv % cand == 0:
            step = cand
    subs = [ATTN_K_ROWS] * (step // ATTN_K_ROWS)
    if step % ATTN_K_ROWS:
        subs.append(step % ATTN_K_ROWS)
    return step, tuple(subs)


def _attn_kernel(par_ref, qc_ref, qn_ref, kc_ref, kn_ref, vt_ref, g_ref, o_ref,
                 s_sc, mx_sc, m_sc, l_sc, acc_sc, *, lam_init, subs):
    i = pl.program_id(2)
    j = pl.program_id(3)
    tq = qc_ref.shape[0]
    n_query_chunks = tq // MXU_DEPTH
    starts = [sum(subs[:u]) for u in range(len(subs))]

    def scores_piece(k_src_ref, row0, n_rows, q_src_ref, comp, qc):
        cols = slice(comp * HEAD_DIM, (comp + 1) * HEAD_DIM)
        qs = slice(qc * MXU_DEPTH, (qc + 1) * MXU_DEPTH)
        st = lax.dot_general(k_src_ref[row0:row0 + n_rows, cols], q_src_ref[qs, cols], (((1,), (1,)), ((), ())),
                             preferred_element_type=jnp.float32)
        s_sc[comp, 0:n_rows, qs] = st
        mx_sc[comp, :, qs] = jnp.max(st.reshape(n_rows // SUBLANES, SUBLANES, MXU_DEPTH), axis=0)

    def block_stats(comp):
        m_blk = jnp.max(mx_sc[comp], axis=0, keepdims=True)
        m_prev = m_sc[comp]
        m_new = jnp.maximum(m_prev, m_blk)
        alpha = jnp.exp2(m_prev - m_new)
        m_sc[comp] = m_new
        return m_new, alpha

    def consume_piece(row0, n_rows, comp, qc, m_new, alpha):
        qs = slice(qc * MXU_DEPTH, (qc + 1) * MXU_DEPTH)
        st = s_sc[comp, 0:n_rows, qs].reshape(n_rows // SUBLANES, SUBLANES, MXU_DEPTH)
        p = jnp.exp2(st - jnp.broadcast_to(m_new[:, qs], (SUBLANES, MXU_DEPTH)))
        a = alpha[:, qs]
        l_sc[comp, :, qs] = a * l_sc[comp, :, qs] + jnp.sum(p, axis=0)
        pv = jnp.dot(vt_ref[:, row0:row0 + n_rows], p.reshape(n_rows, MXU_DEPTH).astype(jnp.bfloat16),
                     preferred_element_type=jnp.float32)
        acc_sc[comp, :, qs] = a * acc_sc[comp, :, qs] + pv

    def sub_step(u):
        if u + 1 < len(subs):
            look_ahead = (kc_ref, starts[u + 1], subs[u + 1], qc_ref)
        else:
            look_ahead = (kn_ref, 0, subs[0], qn_ref)
        stats = [block_stats(comp) for comp in range(2)]
        order = [(comp, qc) for qc in range(n_query_chunks) for comp in range(2)]
        for n, (comp, qc) in enumerate(order):
            consume_piece(starts[u], subs[u], comp, qc, *stats[comp])
            if n > 0:
                scores_piece(*look_ahead, *order[n - 1])
        scores_piece(*look_ahead, *order[-1])

    @pl.when((i == 0) & (j == 0))
    def _():
        for comp in range(2):
            for qc in range(n_query_chunks):
                scores_piece(kc_ref, 0, subs[0], qc_ref, comp, qc)

    @pl.when(j == 0)
    def _():
        m_sc[...] = jnp.full(m_sc.shape, -jnp.inf, jnp.float32)
        l_sc[...] = jnp.zeros(l_sc.shape, jnp.float32)
        acc_sc[...] = jnp.zeros(acc_sc.shape, jnp.float32)

    for u in range(len(subs)):
        sub_step(u)

    @pl.when(j == pl.num_programs(3) - 1)
    def _():
        lq1, lk1, lq2, lk2 = (par_ref[r:r + 1, 0:HEAD_DIM] for r in range(4))
        lam = (jnp.exp(jnp.sum(lq1 * lk1, axis=-1, keepdims=True))
               - jnp.exp(jnp.sum(lq2 * lk2, axis=-1, keepdims=True)) + lam_init)
        l1 = jnp.sum(l_sc[0], axis=0, keepdims=True)
        l2 = jnp.sum(l_sc[1], axis=0, keepdims=True)
        ot = acc_sc[0] / l1 - lam * (acc_sc[1] / l2)
        ot = ot * lax.rsqrt(jnp.mean(ot * ot, axis=0, keepdims=True) + SUBLN_EPS)
        o = ot.T * par_ref[4:5, :] * (1.0 - lam_init)
        o_ref[...] = (o * _silu(g_ref[...])).astype(o_ref.dtype)


def _attention(qk, vt, gate_path, lams, subln_g, lam_init, q_rows, kv_rows):
    _, b, nh, _, _ = qk.shape
    (q_start, sq), (kv_start, skv) = q_rows, kv_rows
    tq = min(ATTN_Q_ROWS, sq)
    tk, subs = _key_plan(skv)
    assert sq % tq == 0 and q_start % tq == 0 and kv_start % tk == 0
    nq, nk = sq // tq, skv // tk
    q0, k0 = q_start // tq, kv_start // tk
    pad = jnp.zeros((1, V_DIM - HEAD_DIM), jnp.float32)
    params = jnp.concatenate([jnp.concatenate([v, pad], axis=1) for v in lams] + [subln_g]
                             + [jnp.zeros((SUBLANES - 5, V_DIM), jnp.float32)], axis=0)
    return pl.pallas_call(
        functools.partial(_attn_kernel, lam_init=lam_init, subs=subs),
        grid=(b, nh, nq, nk),
        in_specs=[
            pl.BlockSpec((SUBLANES, V_DIM), lambda bb, h, i, j: (0, 0)),
            pl.BlockSpec((None, None, None, tq, V_DIM), lambda bb, h, i, j: (0, bb, h, q0 + i, 0)),
            pl.BlockSpec((None, None, None, tq, V_DIM),
                         lambda bb, h, i, j: (0, bb, h, q0 + jnp.minimum(i + (j + 1) // nk, nq - 1), 0)),
            pl.BlockSpec((None, None, None, tk, V_DIM), lambda bb, h, i, j: (1, bb, h, k0 + j, 0)),
            pl.BlockSpec((None, None, None, tk, V_DIM), lambda bb, h, i, j: (1, bb, h, k0 + (j + 1) % nk, 0)),
            pl.BlockSpec((None, None, V_DIM, tk), lambda bb, h, i, j: (bb, h, 0, k0 + j)),
            pl.BlockSpec((None, tq, V_DIM), lambda bb, h, i, j: (bb, i, h)),
        ],
        out_specs=pl.BlockSpec((None, tq, V_DIM), lambda bb, h, i, j: (bb, i, h)),
        out_shape=jax.ShapeDtypeStruct((b, sq, nh * V_DIM), jnp.bfloat16),
        scratch_shapes=[
            pltpu.VMEM((2, max(subs), tq), jnp.float32),
            pltpu.VMEM((2, SUBLANES, tq), jnp.float32),
            pltpu.VMEM((2, 1, tq), jnp.float32),
            pltpu.VMEM((2, SUBLANES, tq), jnp.float32),
            pltpu.VMEM((2, V_DIM, tq), jnp.float32),
        ],
        compiler_params=_params("parallel", "parallel", "arbitrary", "arbitrary"),
        name="diff_attention",
    )(params, qk, qk, qk, qk, vt, gate_path)


def _residual_layer_norm(x, gate, y, ln_g, ln_b, alpha):
    z = alpha * x + gate * y
    mu = jnp.mean(z, axis=-1, keepdims=True)
    zc = z - mu
    var = jnp.mean(zc * zc, axis=-1, keepdims=True)
    return zc * lax.rsqrt(var + LN_EPS) * ln_g + ln_b


def _out_kernel(y_ref, x_ref, gate_ref, w_ref, lg_ref, lb_ref, o_ref, *, alpha):
    half = y_ref.shape[0] // 2
    for rows in (slice(0, half), slice(half, 2 * half)):
        y = jnp.dot(y_ref[rows, :], w_ref[...], preferred_element_type=jnp.float32)
        o_ref[rows, :] = _residual_layer_norm(x_ref[rows, :], gate_ref[...], y, lg_ref[...], lb_ref[...], alpha)


def _output_layer_norm(y, x, gate, w_out, ln_g, ln_b, alpha):
    b, s, d = x.shape
    tm = min(PROJ_ROWS, s)
    vec = pl.BlockSpec((1, d), lambda bb, i: (0, 0))
    return pl.pallas_call(
        functools.partial(_out_kernel, alpha=alpha),
        grid=(b, s // tm),
        in_specs=[
            pl.BlockSpec((None, tm, y.shape[2]), lambda bb, i: (bb, i, 0)),
            pl.BlockSpec((None, tm, d), lambda bb, i: (bb, i, 0)),
            pl.BlockSpec((None, 1, d), lambda bb, i: (bb, 0, 0)),
            pl.BlockSpec(w_out.shape, lambda bb, i: (0, 0)),
            vec, vec,
        ],
        out_specs=pl.BlockSpec((None, tm, d), lambda bb, i: (bb, i, 0)),
        out_shape=jax.ShapeDtypeStruct((b, s, d), jnp.float32),
        compiler_params=_params("parallel", "parallel"),
        name="output_layer_norm",
    )(y, x, gate, w_out, ln_g, ln_b)


def _pool_kernel(u_ref, up_ref, un_ref, g_ref, x_ref, gate_ref, gw_ref, cs_ref, w_ref, lg_ref, lb_ref,
                 o_ref, ubuf, mix, *, seq, alpha):
    i = pl.program_id(1)
    tm = u_ref.shape[0]
    gdim = u_ref.shape[1] // len(POOL_WINDOWS)
    ubuf[0:POOL_HALO, :] = jnp.where(i > 0, up_ref[...], 0.0)
    ubuf[POOL_HALO:POOL_HALO + tm, :] = u_ref[...]
    ubuf[POOL_HALO + tm:2 * POOL_HALO + tm, :] = jnp.where(i < pl.num_programs(1) - 1, un_ref[...], 0.0)
    t = i * tm + lax.broadcasted_iota(jnp.int32, (tm, 1), 0)
    rows = tm + 2 * POOL_HALO
    for gi, w in enumerate(POOL_WINDOWS):
        lo = w // 2
        hi = w - 1 - lo
        assert hi == lo - 1 and lo & (lo - 1) == 0 and lo <= POOL_HALO
        cols = slice(gi * gdim, (gi + 1) * gdim)
        fwd = ubuf[:, cols]
        span = 1
        while span < lo:
            fwd = fwd + pltpu.roll(fwd, rows - span, 0)
            span *= 2
        win = (fwd + pltpu.roll(fwd, lo, 0))[POOL_HALO:POOL_HALO + tm]
        cnt = (jnp.minimum(t + hi + 1, seq) - jnp.maximum(t - lo, 0)).astype(jnp.float32)
        pooled = win / cnt - u_ref[:, cols]
        mix[:, cols] = jnp.dot(pooled.astype(jnp.bfloat16), gw_ref[gi], preferred_element_type=jnp.float32)
    gated = ((mix[...] * cs_ref[...]) * _silu(g_ref[...])).astype(jnp.bfloat16)
    y = jnp.dot(gated, w_ref[...], preferred_element_type=jnp.float32)
    o_ref[...] = _residual_layer_norm(x_ref[...], gate_ref[...], y, lg_ref[...], lb_ref[...], alpha)


def _pool_mix_output(ug, x, gate, grp_w, ch_scale, w_out, ln_g, ln_b, alpha):
    b, s, d = x.shape
    width = ug.shape[2] // 2
    tm = min(PROJ_ROWS, s)
    halo_blocks = tm // POOL_HALO
    last_halo = s // POOL_HALO - 1
    vec = pl.BlockSpec((1, d), lambda bb, i: (0, 0))
    return pl.pallas_call(
        functools.partial(_pool_kernel, seq=s, alpha=alpha),
        grid=(b, s // tm),
        in_specs=[
            pl.BlockSpec((None, tm, width), lambda bb, i: (bb, i, 0)),
            pl.BlockSpec((None, POOL_HALO, width), lambda bb, i: (bb, jnp.maximum(i * halo_blocks - 1, 0), 0)),
            pl.BlockSpec((None, POOL_HALO, width),
                         lambda bb, i: (bb, jnp.minimum((i + 1) * halo_blocks, last_halo), 0)),
            pl.BlockSpec((None, tm, width), lambda bb, i: (bb, i, 1)),
            pl.BlockSpec((None, tm, d), lambda bb, i: (bb, i, 0)),
            pl.BlockSpec((None, 1, d), lambda bb, i: (bb, 0, 0)),
            pl.BlockSpec(grp_w.shape, lambda bb, i: (0, 0, 0)),
            pl.BlockSpec((1, width), lambda bb, i: (0, 0)),
            pl.BlockSpec(w_out.shape, lambda bb, i: (0, 0)),
            vec, vec,
        ],
        out_specs=pl.BlockSpec((None, tm, d), lambda bb, i: (bb, i, 0)),
        out_shape=jax.ShapeDtypeStruct((b, s, d), jnp.float32),
        scratch_shapes=[
            pltpu.VMEM((tm + 2 * POOL_HALO, width), jnp.float32),
            pltpu.VMEM((tm, width), jnp.float32),
        ],
        compiler_params=_params("parallel", "parallel"),
        name="pool_mix_output",
    )(ug, ug, ug, ug, x, gate, grp_w, ch_scale, w_out, ln_g, ln_b)


def _rope_tables(s, total_rows, q_scale):
    n_grid_rows = s // GRID_W
    inv_freq = ROPE_THETA ** (-jnp.arange(ROPE_PAIRS, dtype=jnp.float32) / ROPE_PAIRS)
    ang_r = jnp.arange(n_grid_rows, dtype=jnp.float32)[:, None] * inv_freq
    ang_c = jnp.arange(GRID_W, dtype=jnp.float32)[:, None] * inv_freq
    cr, sr = (jnp.repeat(f(ang_r), GRID_W, axis=0) for f in (jnp.cos, jnp.sin))
    cc, sc = (jnp.tile(f(ang_c), (n_grid_rows, 1)) for f in (jnp.cos, jnp.sin))
    c = jnp.concatenate([cr, cc, cr, cc], axis=-1)
    sn = jnp.concatenate([-sr, -sc, sr, sc], axis=-1)
    rest = total_rows - s
    c = jnp.concatenate([c, jnp.ones((rest, LANES), jnp.float32)], axis=0)
    sn = jnp.concatenate([sn, jnp.zeros((rest, LANES), jnp.float32)], axis=0)
    return tuple(jnp.stack([tab * q_scale, tab]) for tab in (c, sn))


def _pair_rotation_partners(w):
    d, n = w.shape
    w = w.reshape(d, n // HEAD_DIM, 2, 2, ROPE_PAIRS)
    return jnp.swapaxes(w, 2, 3).reshape(d, n)


def kernel(x, c, ctx, c_ctx, mod_w, mod_b, ln_g, ln_b, attn_w_in, attn_w_out, attn_lq1, attn_lk1, attn_lq2,
           attn_lk2, attn_subln_g, pool_w_in, pool_grp_w, pool_scale, pool_w_out):
    b, s, d = x.shape
    ctx_len = ctx.shape[1]
    depth = mod_w.shape[0]
    assert d == N_HEADS * V_DIM and b + 1 <= MOD_ROWS
    assert s % PROJ_ROWS == 0 and ctx_len <= PROJ_ROWS
    alpha = (2 * depth) ** 0.25
    q_scale = LOG2E / math.sqrt(HEAD_DIM)
    bf16 = jnp.bfloat16

    cvec = jnp.concatenate([c, c_ctx[None, :], jnp.zeros((MOD_ROWS - b - 1, d), jnp.float32)], axis=0)
    mod = _modulation(cvec, mod_w, mod_b)
    tabs = _rope_tables(s, s + PROJ_ROWS, q_scale)

    for i in range(depth):
        is_attn = i % 2 == 0
        ctx_out = any(j % 2 == 0 for j in range(i + 1, depth))
        shift, scale, gate = (mod[i, :, k * d:(k + 1) * d] for k in range(3))
        sh_x, sc_x, gt_x = (v[:b, None, :] for v in (shift, scale, gate))
        sh_c, sc_c, gt_c = (jnp.broadcast_to(v[b][None, None, :], (b, 1, d)) for v in (shift, scale, gate))
        lg, lb = ln_g[i][None, :], ln_b[i][None, :]
        if is_attn:
            a = i // 2
            lam_init = 0.8 - 0.6 * math.exp(-0.3 * i)
            w_in = attn_w_in[a].astype(bf16)
            w_qk, w_vt, w_gate = _pair_rotation_partners(w_in[:, :2 * d]), w_in[:, 2 * d:3 * d].T, w_in[:, 3 * d:]
            w_out = attn_w_out[a].astype(bf16)
            lams = [v[a][None, :] for v in (attn_lq1, attn_lk1, attn_lq2, attn_lk2)]
            subg = attn_subln_g[a][None, :]
            total = s + ctx_len
            sh_2, sc_2 = (jnp.stack([vx, vc], axis=1) for vx, vc in ((sh_x, sh_c), (sc_x, sc_c)))
            ctx_pad = jnp.pad(ctx, ((0, 0), (0, PROJ_ROWS - ctx_len), (0, 0)))
            qk = _project_qk(x, ctx_pad, sh_2, sc_2, w_qk, tabs)
            vt = _project_vt(x, ctx_pad, sh_2, sc_2, w_vt)
            g_x = _project_f32(x, sh_x, sc_x, w_gate)
            y_x = _attention(qk, vt, g_x, lams, subg, lam_init, (0, s), (0, total))
            if ctx_out:
                g_c = _project_f32(ctx, sh_c, sc_c, w_gate)
                y_c = _attention(qk, vt, g_c, lams, subg, lam_init, (s, ctx_len), (s, ctx_len))
                ctx = _output_layer_norm(y_c, ctx, gt_c, w_out, lg, lb, alpha)
            x = _output_layer_norm(y_x, x, gt_x, w_out, lg, lb, alpha)
        else:
            p = i // 2
            w_in = pool_w_in[p].astype(bf16)
            grp_w = pool_grp_w[p].astype(bf16)
            w_out = pool_w_out[p].astype(bf16)
            cs = pool_scale[p][None, :]
            ug_x = _project_f32(x, sh_x, sc_x, w_in)
            if ctx_out:
                ug_c = _project_f32(ctx, sh_c, sc_c, w_in)
                ctx = _pool_mix_output(ug_c, ctx, gt_c, grp_w, cs, w_out, lg, lb, alpha)
            x = _pool_mix_output(ug_x, x, gt_x, grp_w, cs, w_out, lg, lb, alpha)
    return x
```

```python
import functools
import math

import jax
import jax.numpy as jnp
from jax import lax
from jax.experimental import pallas as pl
from jax.experimental.pallas import tpu as pltpu

N_HEADS = 8
HEAD_DIM = 128
V_DIM = 2 * HEAD_DIM
GRID_W = 64
ROPE_THETA = 10000.0
ROPE_PAIRS = HEAD_DIM // 4
POOL_WINDOWS = (2, 4, 8, 16)
POOL_HALO = 8
LN_EPS = 1e-5
SUBLN_EPS = 1e-5

LANES = 128
SUBLANES = 8
MXU_DEPTH = 256
VMEM_LIMIT_BYTES = 60 * 1024 * 1024

PROJ_ROWS = 512
PROJ_COLS = 2048
ATTN_Q_ROWS = 2048
ATTN_K_ROWS = 1280
ATTN_KEY_STEP = 3328
MOD_COLS = 1024
MOD_ROWS = SUBLANES

LOG2E = 1.4426950408889634


def _silu(v):
    return v / (1.0 + jnp.exp(-v))


def _params(*semantics):
    return pltpu.CompilerParams(dimension_semantics=semantics, vmem_limit_bytes=VMEM_LIMIT_BYTES)


def _mod_kernel(c_ref, w_ref, b_ref, o_ref):
    s = _silu(c_ref[...])
    o_ref[...] = jnp.dot(s, w_ref[...], precision=lax.Precision.HIGHEST,
                         preferred_element_type=jnp.float32) + b_ref[...]


def _modulation(cvec, mod_w, mod_b):
    depth, d, n = mod_w.shape
    tn = min(MOD_COLS, n)
    return pl.pallas_call(
        _mod_kernel,
        grid=(depth, n // tn),
        in_specs=[
            pl.BlockSpec((MOD_ROWS, d), lambda l, j: (0, 0)),
            pl.BlockSpec((None, d, tn), lambda l, j: (l, 0, j)),
            pl.BlockSpec((None, 1, tn), lambda l, j: (l, 0, j)),
        ],
        out_specs=pl.BlockSpec((None, MOD_ROWS, tn), lambda l, j: (l, 0, j)),
        out_shape=jax.ShapeDtypeStruct((depth, MOD_ROWS, n), jnp.float32),
        compiler_params=_params("parallel", "parallel"),
        name="modulation",
    )(cvec, mod_w, mod_b.reshape(depth, 1, n))


def _modulated_matmul(x_ref, sh_ref, sc_ref, w_ref):
    h = (x_ref[...] * (1.0 + sc_ref[...]) + sh_ref[...]).astype(jnp.bfloat16)
    return jnp.dot(h, w_ref[...], preferred_element_type=jnp.float32)


def _proj_f32_kernel(x_ref, sh_ref, sc_ref, w_ref, o_ref):
    o_ref[...] = _modulated_matmul(x_ref, sh_ref, sc_ref, w_ref)


def _modulated_tokens(x_ref, ctx_ref, sh_ref, sc_ref, step, n_latent_steps):
    tokens = jnp.where(step == n_latent_steps, ctx_ref[...], x_ref[...])
    return (tokens * (1.0 + sc_ref[...]) + sh_ref[...]).astype(jnp.bfloat16)


def _token_specs(tm, d, n_latent_steps, grid_rank):
    def lift(fn):
        return (lambda n, bb, i: fn(bb, i)) if grid_rank == 3 else fn
    return [
        pl.BlockSpec((None, tm, d), lift(lambda bb, i: (bb, jnp.minimum(i, n_latent_steps - 1), 0))),
        pl.BlockSpec((None, tm, d), lift(lambda bb, i: (bb, 0, 0))),
        pl.BlockSpec((None, None, 1, d), lift(lambda bb, i: (bb, i // n_latent_steps, 0, 0))),
        pl.BlockSpec((None, None, 1, d), lift(lambda bb, i: (bb, i // n_latent_steps, 0, 0))),
    ]


def _proj_qk_kernel(x_ref, ctx_ref, sh_ref, sc_ref, w_ref, c_ref, s_ref, o_ref):
    h = _modulated_tokens(x_ref, ctx_ref, sh_ref, sc_ref, pl.program_id(2), pl.num_programs(2) - 1)
    acc = jnp.dot(h, w_ref[...], preferred_element_type=jnp.float32)
    c, s = c_ref[...], s_ref[...]
    for gi in range(acc.shape[1] // LANES):
        t = acc[:, gi * LANES:(gi + 1) * LANES]
        r = t * c + pltpu.roll(t, LANES // 2, 1) * s
        head, comp = divmod(gi, V_DIM // LANES)
        o_ref[head, :, comp * LANES:(comp + 1) * LANES] = r.astype(o_ref.dtype)


def _project_qk(x, ctx, shift, scale, w_qk, tables):
    b, s, d = x.shape
    tn = PROJ_COLS
    nt = w_qk.shape[1] // tn
    nh = tn // V_DIM
    tm = PROJ_ROWS
    nx = s // tm
    tab_spec = pl.BlockSpec((None, tm, LANES), lambda n, bb, i: (n, i, 0))
    return pl.pallas_call(
        _proj_qk_kernel,
        grid=(nt, b, nx + 1),
        in_specs=_token_specs(tm, d, nx, 3) + [
            pl.BlockSpec((d, tn), lambda n, bb, i: (0, n)),
            tab_spec, tab_spec,
        ],
        out_specs=pl.BlockSpec((None, None, nh, tm, V_DIM), lambda n, bb, i: (n, bb, 0, i, 0)),
        out_shape=jax.ShapeDtypeStruct((nt, b, nh, (nx + 1) * tm, V_DIM), jnp.bfloat16),
        compiler_params=_params("arbitrary", "arbitrary", "arbitrary"),
        name="project_qk",
    )(x, ctx, shift, scale, w_qk, *tables)


def _proj_vt_kernel(x_ref, ctx_ref, sh_ref, sc_ref, wt_ref, o_ref):
    h = _modulated_tokens(x_ref, ctx_ref, sh_ref, sc_ref, pl.program_id(1), pl.num_programs(1) - 1)
    vt = lax.dot_general(wt_ref[...], h, (((1,), (1,)), ((), ())),
                         preferred_element_type=jnp.float32)
    for head in range(o_ref.shape[0]):
        o_ref[head] = vt[head * V_DIM:(head + 1) * V_DIM, :].astype(o_ref.dtype)


def _project_vt(x, ctx, shift, scale, w_vt):
    b, s, d = x.shape
    nh = w_vt.shape[0] // V_DIM
    tm = PROJ_ROWS
    nx = s // tm
    return pl.pallas_call(
        _proj_vt_kernel,
        grid=(b, nx + 1),
        in_specs=_token_specs(tm, d, nx, 2) + [pl.BlockSpec(w_vt.shape, lambda bb, i: (0, 0))],
        out_specs=pl.BlockSpec((None, nh, V_DIM, tm), lambda bb, i: (bb, 0, 0, i)),
        out_shape=jax.ShapeDtypeStruct((b, nh, V_DIM, (nx + 1) * tm), jnp.bfloat16),
        compiler_params=_params("parallel", "arbitrary"),
        name="project_vt",
    )(x, ctx, shift, scale, w_vt)


def _project_f32(x, shift, scale, w):
    b, s, d = x.shape
    tn = PROJ_COLS
    nt = w.shape[1] // tn
    tm = min(PROJ_ROWS, s)
    return pl.pallas_call(
        _proj_f32_kernel,
        grid=(nt, b, s // tm),
        in_specs=[
            pl.BlockSpec((None, tm, d), lambda n, bb, i: (bb, i, 0)),
            pl.BlockSpec((None, 1, d), lambda n, bb, i: (bb, 0, 0)),
            pl.BlockSpec((None, 1, d), lambda n, bb, i: (bb, 0, 0)),
            pl.BlockSpec((d, tn), lambda n, bb, i: (0, n)),
        ],
        out_specs=pl.BlockSpec((None, tm, tn), lambda n, bb, i: (bb, i, n)),
        out_shape=jax.ShapeDtypeStruct((b, s, nt * tn), jnp.float32),
        compiler_params=_params("arbitrary", "arbitrary", "arbitrary"),
        name="project_f32",
    )(x, shift, scale, w)


def _key_plan(skv):
    step = MXU_DEPTH
    for cand in range(MXU_DEPTH, min(ATTN_KEY_STEP, skv) + 1, MXU_DEPTH):
        if skv % cand == 0:
            step = cand
    subs = [ATTN_K_ROWS] * (step // ATTN_K_ROWS)
    if step % ATTN_K_ROWS:
        subs.append(step % ATTN_K_ROWS)
    return step, tuple(subs)


def _attn_kernel(par_ref, qc_ref, qn_ref, kc_ref, kn_ref, vt_ref, g_ref, o_ref,
                 s_sc, mx_sc, m_sc, l_sc, acc_sc, *, lam_init, subs):
    i = pl.program_id(2)
    j = pl.program_id(3)
    tq = qc_ref.shape[0]
    n_query_chunks = tq // MXU_DEPTH
    starts = [sum(subs[:u]) for u in range(len(subs))]

    def scores_piece(k_src_ref, row0, n_rows, q_src_ref, comp, qc):
        cols = slice(comp * HEAD_DIM, (comp + 1) * HEAD_DIM)
        qs = slice(qc * MXU_DEPTH, (qc + 1) * MXU_DEPTH)
        st = lax.dot_general(k_src_ref[row0:row0 + n_rows, cols], q_src_ref[qs, cols], (((1,), (1,)), ((), ())),
                             preferred_element_type=jnp.float32)
        s_sc[comp, 0:n_rows, qs] = st
        mx_sc[comp, :, qs] = jnp.max(st.reshape(n_rows // SUBLANES, SUBLANES, MXU_DEPTH), axis=0)

    def consume_piece(row0, n_rows, comp, qc):
        qs = slice(qc * MXU_DEPTH, (qc + 1) * MXU_DEPTH)
        m_prev = m_sc[comp, :, qs]
        m_new = jnp.maximum(m_prev, jnp.max(mx_sc[comp, :, qs], axis=0, keepdims=True))
        a = jnp.exp2(m_prev - m_new)
        m_sc[comp, :, qs] = m_new
        st = s_sc[comp, 0:n_rows, qs].reshape(n_rows // SUBLANES, SUBLANES, MXU_DEPTH)
        p = jnp.exp2(st - jnp.broadcast_to(m_new, (SUBLANES, MXU_DEPTH)))
        l_sc[comp, :, qs] = a * l_sc[comp, :, qs] + jnp.sum(p, axis=0)
        pv = jnp.dot(vt_ref[:, row0:row0 + n_rows], p.reshape(n_rows, MXU_DEPTH).astype(jnp.bfloat16),
                     preferred_element_type=jnp.float32)
        acc_sc[comp, :, qs] = a * acc_sc[comp, :, qs] + pv

    def sub_step(u):
        if u + 1 < len(subs):
            look_ahead = (kc_ref, starts[u + 1], subs[u + 1], qc_ref)
        else:
            look_ahead = (kn_ref, 0, subs[0], qn_ref)
        order = [(comp, qc) for qc in range(n_query_chunks) for comp in range(2)]
        for n, (comp, qc) in enumerate(order):
            consume_piece(starts[u], subs[u], comp, qc)
            if n > 0:
                scores_piece(*look_ahead, *order[n - 1])
        scores_piece(*look_ahead, *order[-1])

    @pl.when((i == 0) & (j == 0))
    def _():
        for comp in range(2):
            for qc in range(n_query_chunks):
                scores_piece(kc_ref, 0, subs[0], qc_ref, comp, qc)

    @pl.when(j == 0)
    def _():
        m_sc[...] = jnp.full(m_sc.shape, -jnp.inf, jnp.float32)
        l_sc[...] = jnp.zeros(l_sc.shape, jnp.float32)
        acc_sc[...] = jnp.zeros(acc_sc.shape, jnp.float32)

    for u in range(len(subs)):
        sub_step(u)

    @pl.when(j == pl.num_programs(3) - 1)
    def _():
        lq1, lk1, lq2, lk2 = (par_ref[r:r + 1, 0:HEAD_DIM] for r in range(4))
        lam = (jnp.exp(jnp.sum(lq1 * lk1, axis=-1, keepdims=True))
               - jnp.exp(jnp.sum(lq2 * lk2, axis=-1, keepdims=True)) + lam_init)
        l1 = jnp.sum(l_sc[0], axis=0, keepdims=True)
        l2 = jnp.sum(l_sc[1], axis=0, keepdims=True)
        ot = acc_sc[0] / l1 - lam * (acc_sc[1] / l2)
        ot = ot * lax.rsqrt(jnp.mean(ot * ot, axis=0, keepdims=True) + SUBLN_EPS)
        o = ot.T * par_ref[4:5, :] * (1.0 - lam_init)
        o_ref[...] = (o * _silu(g_ref[...])).astype(o_ref.dtype)


def _attention(qk, vt, gate_path, lams, subln_g, lam_init, q_rows, kv_rows):
    _, b, nh, _, _ = qk.shape
    (q_start, sq), (kv_start, skv) = q_rows, kv_rows
    tq = min(ATTN_Q_ROWS, sq)
    tk, subs = _key_plan(skv)
    assert sq % tq == 0 and q_start % tq == 0 and kv_start % tk == 0
    nq, nk = sq // tq, skv // tk
    q0, k0 = q_start // tq, kv_start // tk
    pad = jnp.zeros((1, V_DIM - HEAD_DIM), jnp.float32)
    params = jnp.concatenate([jnp.concatenate([v, pad], axis=1) for v in lams] + [subln_g]
                             + [jnp.zeros((SUBLANES - 5, V_DIM), jnp.float32)], axis=0)
    return pl.pallas_call(
        functools.partial(_attn_kernel, lam_init=lam_init, subs=subs),
        grid=(b, nh, nq, nk),
        in_specs=[
            pl.BlockSpec((SUBLANES, V_DIM), lambda bb, h, i, j: (0, 0)),
            pl.BlockSpec((None, None, None, tq, V_DIM), lambda bb, h, i, j: (0, bb, h, q0 + i, 0)),
            pl.BlockSpec((None, None, None, tq, V_DIM),
                         lambda bb, h, i, j: (0, bb, h, q0 + jnp.minimum(i + (j + 1) // nk, nq - 1), 0)),
            pl.BlockSpec((None, None, None, tk, V_DIM), lambda bb, h, i, j: (1, bb, h, k0 + j, 0)),
            pl.BlockSpec((None, None, None, tk, V_DIM), lambda bb, h, i, j: (1, bb, h, k0 + (j + 1) % nk, 0)),
            pl.BlockSpec((None, None, V_DIM, tk), lambda bb, h, i, j: (bb, h, 0, k0 + j)),
            pl.BlockSpec((None, tq, V_DIM), lambda bb, h, i, j: (bb, i, h)),
        ],
        out_specs=pl.BlockSpec((None, tq, V_DIM), lambda bb, h, i, j: (bb, i, h)),
        out_shape=jax.ShapeDtypeStruct((b, sq, nh * V_DIM), jnp.bfloat16),
        scratch_shapes=[
            pltpu.VMEM((2, max(subs), tq), jnp.float32),
            pltpu.VMEM((2, SUBLANES, tq), jnp.float32),
            pltpu.VMEM((2, 1, tq), jnp.float32),
            pltpu.VMEM((2, SUBLANES, tq), jnp.float32),
            pltpu.VMEM((2, V_DIM, tq), jnp.float32),
        ],
        compiler_params=_params("parallel", "parallel", "arbitrary", "arbitrary"),
        name="diff_attention",
    )(params, qk, qk, qk, qk, vt, gate_path)


def _residual_layer_norm(x, gate, y, ln_g, ln_b, alpha):
    z = alpha * x + gate * y
    mu = jnp.mean(z, axis=-1, keepdims=True)
    zc = z - mu
    var = jnp.mean(zc * zc, axis=-1, keepdims=True)
    return zc * lax.rsqrt(var + LN_EPS) * ln_g + ln_b


def _out_kernel(y_ref, x_ref, gate_ref, w_ref, lg_ref, lb_ref, o_ref, *, alpha):
    half = y_ref.shape[0] // 2
    for rows in (slice(0, half), slice(half, 2 * half)):
        y = jnp.dot(y_ref[rows, :], w_ref[...], preferred_element_type=jnp.float32)
        o_ref[rows, :] = _residual_layer_norm(x_ref[rows, :], gate_ref[...], y, lg_ref[...], lb_ref[...], alpha)


def _output_layer_norm(y, x, gate, w_out, ln_g, ln_b, alpha):
    b, s, d = x.shape
    tm = min(PROJ_ROWS, s)
    vec = pl.BlockSpec((1, d), lambda bb, i: (0, 0))
    return pl.pallas_call(
        functools.partial(_out_kernel, alpha=alpha),
        grid=(b, s // tm),
        in_specs=[
            pl.BlockSpec((None, tm, y.shape[2]), lambda bb, i: (bb, i, 0)),
            pl.BlockSpec((None, tm, d), lambda bb, i: (bb, i, 0)),
            pl.BlockSpec((None, 1, d), lambda bb, i: (bb, 0, 0)),
            pl.BlockSpec(w_out.shape, lambda bb, i: (0, 0)),
            vec, vec,
        ],
        out_specs=pl.BlockSpec((None, tm, d), lambda bb, i: (bb, i, 0)),
        out_shape=jax.ShapeDtypeStruct((b, s, d), jnp.float32),
        compiler_params=_params("parallel", "parallel"),
        name="output_layer_norm",
    )(y, x, gate, w_out, ln_g, ln_b)


def _pool_kernel(u_ref, up_ref, un_ref, g_ref, x_ref, gate_ref, gw_ref, cs_ref, w_ref, lg_ref, lb_ref,
                 o_ref, ubuf, mix, *, seq, alpha):
    i = pl.program_id(1)
    tm = u_ref.shape[0]
    gdim = u_ref.shape[1] // len(POOL_WINDOWS)
    ubuf[0:POOL_HALO, :] = jnp.where(i > 0, up_ref[...], 0.0)
    ubuf[POOL_HALO:POOL_HALO + tm, :] = u_ref[...]
    ubuf[POOL_HALO + tm:2 * POOL_HALO + tm, :] = jnp.where(i < pl.num_programs(1) - 1, un_ref[...], 0.0)
    t = i * tm + lax.broadcasted_iota(jnp.int32, (tm, 1), 0)
    rows = tm + 2 * POOL_HALO
    for gi, w in enumerate(POOL_WINDOWS):
        lo = w // 2
        hi = w - 1 - lo
        assert hi == lo - 1 and lo & (lo - 1) == 0 and lo <= POOL_HALO
        cols = slice(gi * gdim, (gi + 1) * gdim)
        fwd = ubuf[:, cols]
        span = 1
        while span < lo:
            fwd = fwd + pltpu.roll(fwd, rows - span, 0)
            span *= 2
        win = (fwd + pltpu.roll(fwd, lo, 0))[POOL_HALO:POOL_HALO + tm]
        cnt = (jnp.minimum(t + hi + 1, seq) - jnp.maximum(t - lo, 0)).astype(jnp.float32)
        pooled = win / cnt - u_ref[:, cols]
        mix[:, cols] = jnp.dot(pooled.astype(jnp.bfloat16), gw_ref[gi], preferred_element_type=jnp.float32)
    gated = ((mix[...] * cs_ref[...]) * _silu(g_ref[...])).astype(jnp.bfloat16)
    y = jnp.dot(gated, w_ref[...], preferred_element_type=jnp.float32)
    o_ref[...] = _residual_layer_norm(x_ref[...], gate_ref[...], y, lg_ref[...], lb_ref[...], alpha)


def _pool_mix_output(ug, x, gate, grp_w, ch_scale, w_out, ln_g, ln_b, alpha):
    b, s, d = x.shape
    width = ug.shape[2] // 2
    tm = min(PROJ_ROWS, s)
    halo_blocks = tm // POOL_HALO
    last_halo = s // POOL_HALO - 1
    vec = pl.BlockSpec((1, d), lambda bb, i: (0, 0))
    return pl.pallas_call(
        functools.partial(_pool_kernel, seq=s, alpha=alpha),
        grid=(b, s // tm),
        in_specs=[
            pl.BlockSpec((None, tm, width), lambda bb, i: (bb, i, 0)),
            pl.BlockSpec((None, POOL_HALO, width), lambda bb, i: (bb, jnp.maximum(i * halo_blocks - 1, 0), 0)),
            pl.BlockSpec((None, POOL_HALO, width),
                         lambda bb, i: (bb, jnp.minimum((i + 1) * halo_blocks, last_halo), 0)),
            pl.BlockSpec((None, tm, width), lambda bb, i: (bb, i, 1)),
            pl.BlockSpec((None, tm, d), lambda bb, i: (bb, i, 0)),
            pl.BlockSpec((None, 1, d), lambda bb, i: (bb, 0, 0)),
            pl.BlockSpec(grp_w.shape, lambda bb, i: (0, 0, 0)),
            pl.BlockSpec((1, width), lambda bb, i: (0, 0)),
            pl.BlockSpec(w_out.shape, lambda bb, i: (0, 0)),
            vec, vec,
        ],
        out_specs=pl.BlockSpec((None, tm, d), lambda bb, i: (bb, i, 0)),
        out_shape=jax.ShapeDtypeStruct((b, s, d), jnp.float32),
        scratch_shapes=[
            pltpu.VMEM((tm + 2 * POOL_HALO, width), jnp.float32),
            pltpu.VMEM((tm, width), jnp.float32),
        ],
        compiler_params=_params("parallel", "parallel"),
        name="pool_mix_output",
    )(ug, ug, ug, ug, x, gate, grp_w, ch_scale, w_out, ln_g, ln_b)


def _rope_tables(s, total_rows, q_scale):
    n_grid_rows = s // GRID_W
    inv_freq = ROPE_THETA ** (-jnp.arange(ROPE_PAIRS, dtype=jnp.float32) / ROPE_PAIRS)
    ang_r = jnp.arange(n_grid_rows, dtype=jnp.float32)[:, None] * inv_freq
    ang_c = jnp.arange(GRID_W, dtype=jnp.float32)[:, None] * inv_freq
    cr, sr = (jnp.repeat(f(ang_r), GRID_W, axis=0) for f in (jnp.cos, jnp.sin))
    cc, sc = (jnp.tile(f(ang_c), (n_grid_rows, 1)) for f in (jnp.cos, jnp.sin))
    c = jnp.concatenate([cr, cc, cr, cc], axis=-1)
    sn = jnp.concatenate([-sr, -sc, sr, sc], axis=-1)
    rest = total_rows - s
    c = jnp.concatenate([c, jnp.ones((rest, LANES), jnp.float32)], axis=0)
    sn = jnp.concatenate([sn, jnp.zeros((rest, LANES), jnp.float32)], axis=0)
    return tuple(jnp.stack([tab * q_scale, tab]) for tab in (c, sn))


def _pair_rotation_partners(w):
    d, n = w.shape
    w = w.reshape(d, n // HEAD_DIM, 2, 2, ROPE_PAIRS)
    return jnp.swapaxes(w, 2, 3).reshape(d, n)


def kernel(x, c, ctx, c_ctx, mod_w, mod_b, ln_g, ln_b, attn_w_in, attn_w_out, attn_lq1, attn_lk1, attn_lq2,
           attn_lk2, attn_subln_g, pool_w_in, pool_grp_w, pool_scale, pool_w_out):
    b, s, d = x.shape
    ctx_len = ctx.shape[1]
    depth = mod_w.shape[0]
    assert d == N_HEADS * V_DIM and b + 1 <= MOD_ROWS
    assert s % PROJ_ROWS == 0 and ctx_len <= PROJ_ROWS
    alpha = (2 * depth) ** 0.25
    q_scale = LOG2E / math.sqrt(HEAD_DIM)
    bf16 = jnp.bfloat16

    cvec = jnp.concatenate([c, c_ctx[None, :], jnp.zeros((MOD_ROWS - b - 1, d), jnp.float32)], axis=0)
    mod = _modulation(cvec, mod_w, mod_b)
    tabs = _rope_tables(s, s + PROJ_ROWS, q_scale)

    for i in range(depth):
        is_attn = i % 2 == 0
        ctx_out = any(j % 2 == 0 for j in range(i + 1, depth))
        shift, scale, gate = (mod[i, :, k * d:(k + 1) * d] for k in range(3))
        sh_x, sc_x, gt_x = (v[:b, None, :] for v in (shift, scale, gate))
        sh_c, sc_c, gt_c = (jnp.broadcast_to(v[b][None, None, :], (b, 1, d)) for v in (shift, scale, gate))
        lg, lb = ln_g[i][None, :], ln_b[i][None, :]
        if is_attn:
            a = i // 2
            lam_init = 0.8 - 0.6 * math.exp(-0.3 * i)
            w_in = attn_w_in[a].astype(bf16)
            w_qk, w_vt, w_gate = _pair_rotation_partners(w_in[:, :2 * d]), w_in[:, 2 * d:3 * d].T, w_in[:, 3 * d:]
            w_out = attn_w_out[a].astype(bf16)
            lams = [v[a][None, :] for v in (attn_lq1, attn_lk1, attn_lq2, attn_lk2)]
            subg = attn_subln_g[a][None, :]
            total = s + ctx_len
            sh_2, sc_2 = (jnp.stack([vx, vc], axis=1) for vx, vc in ((sh_x, sh_c), (sc_x, sc_c)))
            ctx_pad = jnp.pad(ctx, ((0, 0), (0, PROJ_ROWS - ctx_len), (0, 0)))
            qk = _project_qk(x, ctx_pad, sh_2, sc_2, w_qk, tabs)
            vt = _project_vt(x, ctx_pad, sh_2, sc_2, w_vt)
            g_x = _project_f32(x, sh_x, sc_x, w_gate)
            y_x = _attention(qk, vt, g_x, lams, subg, lam_init, (0, s), (0, total))
            if ctx_out:
                g_c = _project_f32(ctx, sh_c, sc_c, w_gate)
                y_c = _attention(qk, vt, g_c, lams, subg, lam_init, (s, ctx_len), (s, ctx_len))
                ctx = _output_layer_norm(y_c, ctx, gt_c, w_out, lg, lb, alpha)
            x = _output_layer_norm(y_x, x, gt_x, w_out, lg, lb, alpha)
        else:
            p = i // 2
            w_in = pool_w_in[p].astype(bf16)
            grp_w = pool_grp_w[p].astype(bf16)
            w_out = pool_w_out[p].astype(bf16)
            cs = pool_scale[p][None, :]
            ug_x = _project_f32(x, sh_x, sc_x, w_in)
            if ctx_out:
                ug_c = _project_f32(ctx, sh_c, sc_c, w_in)
                ctx = _pool_mix_output(ug_c, ctx, gt_c, grp_w, cs, w_out, lg, lb, alpha)
            x = _pool_mix_output(ug_x, x, gt_x, grp_w, cs, w_out, lg, lb, alpha)
    return x
```

```python
import functools
import math

import jax
import jax.numpy as jnp
from jax import lax
from jax.experimental import pallas as pl
from jax.experimental.pallas import tpu as pltpu

N_HEADS = 8
HEAD_DIM = 128
V_DIM = 2 * HEAD_DIM
GRID_W = 64
ROPE_THETA = 10000.0
ROPE_PAIRS = HEAD_DIM // 4
POOL_WINDOWS = (2, 4, 8, 16)
POOL_HALO = 8
LN_EPS = 1e-5
SUBLN_EPS = 1e-5

LANES = 128
SUBLANES = 8
MXU_DEPTH = 256
VMEM_LIMIT_BYTES = 60 * 1024 * 1024

PROJ_ROWS = 512
PROJ_COLS = 2048
ATTN_Q_ROWS = 2048
ATTN_K_ROWS = 1280
ATTN_KEY_STEP = 3328
ATTN_DEFERRED_PIECES = 2
MOD_COLS = 1024
MOD_ROWS = SUBLANES

LOG2E = 1.4426950408889634


def _silu(v):
    return v / (1.0 + jnp.exp(-v))


def _params(*semantics):
    return pltpu.CompilerParams(dimension_semantics=semantics, vmem_limit_bytes=VMEM_LIMIT_BYTES)


def _mod_kernel(c_ref, w_ref, b_ref, o_ref):
    s = _silu(c_ref[...])
    o_ref[...] = jnp.dot(s, w_ref[...], precision=lax.Precision.HIGHEST,
                         preferred_element_type=jnp.float32) + b_ref[...]


def _modulation(cvec, mod_w, mod_b):
    depth, d, n = mod_w.shape
    tn = min(MOD_COLS, n)
    return pl.pallas_call(
        _mod_kernel,
        grid=(depth, n // tn),
        in_specs=[
            pl.BlockSpec((MOD_ROWS, d), lambda l, j: (0, 0)),
            pl.BlockSpec((None, d, tn), lambda l, j: (l, 0, j)),
            pl.BlockSpec((None, 1, tn), lambda l, j: (l, 0, j)),
        ],
        out_specs=pl.BlockSpec((None, MOD_ROWS, tn), lambda l, j: (l, 0, j)),
        out_shape=jax.ShapeDtypeStruct((depth, MOD_ROWS, n), jnp.float32),
        compiler_params=_params("parallel", "parallel"),
        name="modulation",
    )(cvec, mod_w, mod_b.reshape(depth, 1, n))


def _modulated_matmul(x_ref, sh_ref, sc_ref, w_ref):
    h = (x_ref[...] * (1.0 + sc_ref[...]) + sh_ref[...]).astype(jnp.bfloat16)
    return jnp.dot(h, w_ref[...], preferred_element_type=jnp.float32)


def _proj_f32_kernel(x_ref, sh_ref, sc_ref, w_ref, o_ref):
    o_ref[...] = _modulated_matmul(x_ref, sh_ref, sc_ref, w_ref)


def _modulated_tokens(x_ref, ctx_ref, sh_ref, sc_ref, step, n_latent_steps):
    tokens = jnp.where(step == n_latent_steps, ctx_ref[...], x_ref[...])
    return (tokens * (1.0 + sc_ref[...]) + sh_ref[...]).astype(jnp.bfloat16)


def _token_specs(tm, d, n_latent_steps, grid_rank):
    def lift(fn):
        return (lambda n, bb, i: fn(bb, i)) if grid_rank == 3 else fn
    return [
        pl.BlockSpec((None, tm, d), lift(lambda bb, i: (bb, jnp.minimum(i, n_latent_steps - 1), 0))),
        pl.BlockSpec((None, tm, d), lift(lambda bb, i: (bb, 0, 0))),
        pl.BlockSpec((None, None, 1, d), lift(lambda bb, i: (bb, i // n_latent_steps, 0, 0))),
        pl.BlockSpec((None, None, 1, d), lift(lambda bb, i: (bb, i // n_latent_steps, 0, 0))),
    ]


def _proj_qk_kernel(x_ref, ctx_ref, sh_ref, sc_ref, w_ref, c_ref, s_ref, o_ref):
    h = _modulated_tokens(x_ref, ctx_ref, sh_ref, sc_ref, pl.program_id(2), pl.num_programs(2) - 1)
    acc = jnp.dot(h, w_ref[...], preferred_element_type=jnp.float32)
    c, s = c_ref[...], s_ref[...]
    for gi in range(acc.shape[1] // LANES):
        t = acc[:, gi * LANES:(gi + 1) * LANES]
        r = t * c + pltpu.roll(t, LANES // 2, 1) * s
        head, comp = divmod(gi, V_DIM // LANES)
        o_ref[head, :, comp * LANES:(comp + 1) * LANES] = r.astype(o_ref.dtype)


def _project_qk(x, ctx, shift, scale, w_qk, tables):
    b, s, d = x.shape
    tn = PROJ_COLS
    nt = w_qk.shape[1] // tn
    nh = tn // V_DIM
    tm = PROJ_ROWS
    nx = s // tm
    tab_spec = pl.BlockSpec((None, tm, LANES), lambda n, bb, i: (n, i, 0))
    return pl.pallas_call(
        _proj_qk_kernel,
        grid=(nt, b, nx + 1),
        in_specs=_token_specs(tm, d, nx, 3) + [
            pl.BlockSpec((d, tn), lambda n, bb, i: (0, n)),
            tab_spec, tab_spec,
        ],
        out_specs=pl.BlockSpec((None, None, nh, tm, V_DIM), lambda n, bb, i: (n, bb, 0, i, 0)),
        out_shape=jax.ShapeDtypeStruct((nt, b, nh, (nx + 1) * tm, V_DIM), jnp.bfloat16),
        compiler_params=_params("arbitrary", "arbitrary", "arbitrary"),
        name="project_qk",
    )(x, ctx, shift, scale, w_qk, *tables)


def _proj_vt_kernel(x_ref, ctx_ref, sh_ref, sc_ref, wt_ref, o_ref):
    h = _modulated_tokens(x_ref, ctx_ref, sh_ref, sc_ref, pl.program_id(1), pl.num_programs(1) - 1)
    vt = lax.dot_general(wt_ref[...], h, (((1,), (1,)), ((), ())),
                         preferred_element_type=jnp.float32)
    for head in range(o_ref.shape[0]):
        o_ref[head] = vt[head * V_DIM:(head + 1) * V_DIM, :].astype(o_ref.dtype)


def _project_vt(x, ctx, shift, scale, w_vt):
    b, s, d = x.shape
    nh = w_vt.shape[0] // V_DIM
    tm = PROJ_ROWS
    nx = s // tm
    return pl.pallas_call(
        _proj_vt_kernel,
        grid=(b, nx + 1),
        in_specs=_token_specs(tm, d, nx, 2) + [pl.BlockSpec(w_vt.shape, lambda bb, i: (0, 0))],
        out_specs=pl.BlockSpec((None, nh, V_DIM, tm), lambda bb, i: (bb, 0, 0, i)),
        out_shape=jax.ShapeDtypeStruct((b, nh, V_DIM, (nx + 1) * tm), jnp.bfloat16),
        compiler_params=_params("parallel", "arbitrary"),
        name="project_vt",
    )(x, ctx, shift, scale, w_vt)


def _project_f32(x, shift, scale, w):
    b, s, d = x.shape
    tn = PROJ_COLS
    nt = w.shape[1] // tn
    tm = min(PROJ_ROWS, s)
    return pl.pallas_call(
        _proj_f32_kernel,
        grid=(nt, b, s // tm),
        in_specs=[
            pl.BlockSpec((None, tm, d), lambda n, bb, i: (bb, i, 0)),
            pl.BlockSpec((None, 1, d), lambda n, bb, i: (bb, 0, 0)),
            pl.BlockSpec((None, 1, d), lambda n, bb, i: (bb, 0, 0)),
            pl.BlockSpec((d, tn), lambda n, bb, i: (0, n)),
        ],
        out_specs=pl.BlockSpec((None, tm, tn), lambda n, bb, i: (bb, i, n)),
        out_shape=jax.ShapeDtypeStruct((b, s, nt * tn), jnp.float32),
        compiler_params=_params("arbitrary", "arbitrary", "arbitrary"),
        name="project_f32",
    )(x, shift, scale, w)


def _key_plan(skv):
    step = MXU_DEPTH
    for cand in range(MXU_DEPTH, min(ATTN_KEY_STEP, skv) + 1, MXU_DEPTH):
        if skv % cand == 0:
            step = cand
    subs = [ATTN_K_ROWS] * (step // ATTN_K_ROWS)
    if step % ATTN_K_ROWS:
        subs.append(step % ATTN_K_ROWS)
    return step, tuple(subs)


def _attn_kernel(par_ref, qc_ref, qn_ref, kc_ref, kn_ref, vt_ref, g_ref, o_ref,
                 s_sc, mx_sc, m_sc, l_sc, acc_sc, *, lam_init, subs):
    i = pl.program_id(2)
    j = pl.program_id(3)
    tq = qc_ref.shape[0]
    n_query_chunks = tq // MXU_DEPTH
    starts = [sum(subs[:u]) for u in range(len(subs))]

    def scores_piece(k_src_ref, row0, n_rows, q_src_ref, comp, qc):
        cols = slice(comp * HEAD_DIM, (comp + 1) * HEAD_DIM)
        qs = slice(qc * MXU_DEPTH, (qc + 1) * MXU_DEPTH)
        st = lax.dot_general(k_src_ref[row0:row0 + n_rows, cols], q_src_ref[qs, cols], (((1,), (1,)), ((), ())),
                             preferred_element_type=jnp.float32)
        s_sc[comp, 0:n_rows, qs] = st
        mx_sc[comp, :, qs] = jnp.max(st.reshape(n_rows // SUBLANES, SUBLANES, MXU_DEPTH), axis=0)

    def consume_piece(row0, n_rows, comp, qc):
        qs = slice(qc * MXU_DEPTH, (qc + 1) * MXU_DEPTH)
        m_prev = m_sc[comp, :, qs]
        m_new = jnp.maximum(m_prev, jnp.max(mx_sc[comp, :, qs], axis=0, keepdims=True))
        a = jnp.exp2(m_prev - m_new)
        m_sc[comp, :, qs] = m_new
        st = s_sc[comp, 0:n_rows, qs].reshape(n_rows // SUBLANES, SUBLANES, MXU_DEPTH)
        p = jnp.exp2(st - jnp.broadcast_to(m_new, (SUBLANES, MXU_DEPTH)))
        l_sc[comp, :, qs] = a * l_sc[comp, :, qs] + jnp.sum(p, axis=0)
        pv = jnp.dot(vt_ref[:, row0:row0 + n_rows], p.reshape(n_rows, MXU_DEPTH).astype(jnp.bfloat16),
                     preferred_element_type=jnp.float32)
        acc_sc[comp, :, qs] = a * acc_sc[comp, :, qs] + pv

    order = [(comp, qc) for qc in range(n_query_chunks) for comp in range(2)]
    n_deferred = min(ATTN_DEFERRED_PIECES, len(order) - 1)
    n_ahead = len(order) - n_deferred

    def sub_step(u):
        if u + 1 < len(subs):
            look_ahead = (kc_ref, starts[u + 1], subs[u + 1], qc_ref)
            n_look = len(order)
        else:
            look_ahead = (kn_ref, 0, subs[0], qn_ref)
            n_look = n_ahead
        for n, (comp, qc) in enumerate(order):
            if u == 0 and n < n_deferred:
                scores_piece(kc_ref, 0, subs[0], qc_ref, *order[n_ahead + n])
            consume_piece(starts[u], subs[u], comp, qc)
            if 0 < n <= n_look:
                scores_piece(*look_ahead, *order[n - 1])
        if n_look == len(order):
            scores_piece(*look_ahead, *order[-1])

    @pl.when((i == 0) & (j == 0))
    def _():
        for comp, qc in order[:n_ahead]:
            scores_piece(kc_ref, 0, subs[0], qc_ref, comp, qc)

    @pl.when(j == 0)
    def _():
        m_sc[...] = jnp.full(m_sc.shape, -jnp.inf, jnp.float32)
        l_sc[...] = jnp.zeros(l_sc.shape, jnp.float32)
        acc_sc[...] = jnp.zeros(acc_sc.shape, jnp.float32)

    for u in range(len(subs)):
        sub_step(u)

    @pl.when(j == pl.num_programs(3) - 1)
    def _():
        lq1, lk1, lq2, lk2 = (par_ref[r:r + 1, 0:HEAD_DIM] for r in range(4))
        lam = (jnp.exp(jnp.sum(lq1 * lk1, axis=-1, keepdims=True))
               - jnp.exp(jnp.sum(lq2 * lk2, axis=-1, keepdims=True)) + lam_init)
        l1 = jnp.sum(l_sc[0], axis=0, keepdims=True)
        l2 = jnp.sum(l_sc[1], axis=0, keepdims=True)
        ot = acc_sc[0] / l1 - lam * (acc_sc[1] / l2)
        ot = ot * lax.rsqrt(jnp.mean(ot * ot, axis=0, keepdims=True) + SUBLN_EPS)
        o = ot.T * par_ref[4:5, :] * (1.0 - lam_init)
        o_ref[...] = (o * _silu(g_ref[...])).astype(o_ref.dtype)


def _attention(qk, vt, gate_path, lams, subln_g, lam_init, q_rows, kv_rows):
    _, b, nh, _, _ = qk.shape
    (q_start, sq), (kv_start, skv) = q_rows, kv_rows
    tq = min(ATTN_Q_ROWS, sq)
    tk, subs = _key_plan(skv)
    assert sq % tq == 0 and q_start % tq == 0 and kv_start % tk == 0
    nq, nk = sq // tq, skv // tk
    q0, k0 = q_start // tq, kv_start // tk
    pad = jnp.zeros((1, V_DIM - HEAD_DIM), jnp.float32)
    params = jnp.concatenate([jnp.concatenate([v, pad], axis=1) for v in lams] + [subln_g]
                             + [jnp.zeros((SUBLANES - 5, V_DIM), jnp.float32)], axis=0)
    return pl.pallas_call(
        functools.partial(_attn_kernel, lam_init=lam_init, subs=subs),
        grid=(b, nh, nq, nk),
        in_specs=[
            pl.BlockSpec((SUBLANES, V_DIM), lambda bb, h, i, j: (0, 0)),
            pl.BlockSpec((None, None, None, tq, V_DIM), lambda bb, h, i, j: (0, bb, h, q0 + i, 0)),
            pl.BlockSpec((None, None, None, tq, V_DIM),
                         lambda bb, h, i, j: (0, bb, h, q0 + jnp.minimum(i + (j + 1) // nk, nq - 1), 0)),
            pl.BlockSpec((None, None, None, tk, V_DIM), lambda bb, h, i, j: (1, bb, h, k0 + j, 0)),
            pl.BlockSpec((None, None, None, tk, V_DIM), lambda bb, h, i, j: (1, bb, h, k0 + (j + 1) % nk, 0)),
            pl.BlockSpec((None, None, V_DIM, tk), lambda bb, h, i, j: (bb, h, 0, k0 + j)),
            pl.BlockSpec((None, tq, V_DIM), lambda bb, h, i, j: (bb, i, h)),
        ],
        out_specs=pl.BlockSpec((None, tq, V_DIM), lambda bb, h, i, j: (bb, i, h)),
        out_shape=jax.ShapeDtypeStruct((b, sq, nh * V_DIM), jnp.bfloat16),
        scratch_shapes=[
            pltpu.VMEM((2, max(subs), tq), jnp.float32),
            pltpu.VMEM((2, SUBLANES, tq), jnp.float32),
            pltpu.VMEM((2, 1, tq), jnp.float32),
            pltpu.VMEM((2, SUBLANES, tq), jnp.float32),
            pltpu.VMEM((2, V_DIM, tq), jnp.float32),
        ],
        compiler_params=_params("parallel", "parallel", "arbitrary", "arbitrary"),
        name="diff_attention",
    )(params, qk, qk, qk, qk, vt, gate_path)


def _residual_layer_norm(x, gate, y, ln_g, ln_b, alpha):
    z = alpha * x + gate * y
    mu = jnp.mean(z, axis=-1, keepdims=True)
    zc = z - mu
    var = jnp.mean(zc * zc, axis=-1, keepdims=True)
    return zc * lax.rsqrt(var + LN_EPS) * ln_g + ln_b


def _out_kernel(y_ref, x_ref, gate_ref, w_ref, lg_ref, lb_ref, o_ref, *, alpha):
    half = y_ref.shape[0] // 2
    for rows in (slice(0, half), slice(half, 2 * half)):
        y = jnp.dot(y_ref[rows, :], w_ref[...], preferred_element_type=jnp.float32)
        o_ref[rows, :] = _residual_layer_norm(x_ref[rows, :], gate_ref[...], y, lg_ref[...], lb_ref[...], alpha)


def _output_layer_norm(y, x, gate, w_out, ln_g, ln_b, alpha):
    b, s, d = x.shape
    tm = min(PROJ_ROWS, s)
    vec = pl.BlockSpec((1, d), lambda bb, i: (0, 0))
    return pl.pallas_call(
        functools.partial(_out_kernel, alpha=alpha),
        grid=(b, s // tm),
        in_specs=[
            pl.BlockSpec((None, tm, y.shape[2]), lambda bb, i: (bb, i, 0)),
            pl.BlockSpec((None, tm, d), lambda bb, i: (bb, i, 0)),
            pl.BlockSpec((None, 1, d), lambda bb, i: (bb, 0, 0)),
            pl.BlockSpec(w_out.shape, lambda bb, i: (0, 0)),
            vec, vec,
        ],
        out_specs=pl.BlockSpec((None, tm, d), lambda bb, i: (bb, i, 0)),
        out_shape=jax.ShapeDtypeStruct((b, s, d), jnp.float32),
        compiler_params=_params("parallel", "parallel"),
        name="output_layer_norm",
    )(y, x, gate, w_out, ln_g, ln_b)


def _pool_kernel(u_ref, up_ref, un_ref, g_ref, x_ref, gate_ref, gw_ref, cs_ref, w_ref, lg_ref, lb_ref,
                 o_ref, ubuf, mix, *, seq, alpha):
    i = pl.program_id(1)
    tm = u_ref.shape[0]
    gdim = u_ref.shape[1] // len(POOL_WINDOWS)
    ubuf[0:POOL_HALO, :] = jnp.where(i > 0, up_ref[...], 0.0)
    ubuf[POOL_HALO:POOL_HALO + tm, :] = u_ref[...]
    ubuf[POOL_HALO + tm:2 * POOL_HALO + tm, :] = jnp.where(i < pl.num_programs(1) - 1, un_ref[...], 0.0)
    t = i * tm + lax.broadcasted_iota(jnp.int32, (tm, 1), 0)
    rows = tm + 2 * POOL_HALO
    for gi, w in enumerate(POOL_WINDOWS):
        lo = w // 2
        hi = w - 1 - lo
        assert hi == lo - 1 and lo & (lo - 1) == 0 and lo <= POOL_HALO
        cols = slice(gi * gdim, (gi + 1) * gdim)
        fwd = ubuf[:, cols]
        span = 1
        while span < lo:
            fwd = fwd + pltpu.roll(fwd, rows - span, 0)
            span *= 2
        win = (fwd + pltpu.roll(fwd, lo, 0))[POOL_HALO:POOL_HALO + tm]
        cnt = (jnp.minimum(t + hi + 1, seq) - jnp.maximum(t - lo, 0)).astype(jnp.float32)
        pooled = win / cnt - u_ref[:, cols]
        mix[:, cols] = jnp.dot(pooled.astype(jnp.bfloat16), gw_ref[gi], preferred_element_type=jnp.float32)
    gated = ((mix[...] * cs_ref[...]) * _silu(g_ref[...])).astype(jnp.bfloat16)
    y = jnp.dot(gated, w_ref[...], preferred_element_type=jnp.float32)
    o_ref[...] = _residual_layer_norm(x_ref[...], gate_ref[...], y, lg_ref[...], lb_ref[...], alpha)


def _pool_mix_output(ug, x, gate, grp_w, ch_scale, w_out, ln_g, ln_b, alpha):
    b, s, d = x.shape
    width = ug.shape[2] // 2
    tm = min(PROJ_ROWS, s)
    halo_blocks = tm // POOL_HALO
    last_halo = s // POOL_HALO - 1
    vec = pl.BlockSpec((1, d), lambda bb, i: (0, 0))
    return pl.pallas_call(
        functools.partial(_pool_kernel, seq=s, alpha=alpha),
        grid=(b, s // tm),
        in_specs=[
            pl.BlockSpec((None, tm, width), lambda bb, i: (bb, i, 0)),
            pl.BlockSpec((None, POOL_HALO, width), lambda bb, i: (bb, jnp.maximum(i * halo_blocks - 1, 0), 0)),
            pl.BlockSpec((None, POOL_HALO, width),
                         lambda bb, i: (bb, jnp.minimum((i + 1) * halo_blocks, last_halo), 0)),
            pl.BlockSpec((None, tm, width), lambda bb, i: (bb, i, 1)),
            pl.BlockSpec((None, tm, d), lambda bb, i: (bb, i, 0)),
            pl.BlockSpec((None, 1, d), lambda bb, i: (bb, 0, 0)),
            pl.BlockSpec(grp_w.shape, lambda bb, i: (0, 0, 0)),
            pl.BlockSpec((1, width), lambda bb, i: (0, 0)),
            pl.BlockSpec(w_out.shape, lambda bb, i: (0, 0)),
            vec, vec,
        ],
        out_specs=pl.BlockSpec((None, tm, d), lambda bb, i: (bb, i, 0)),
        out_shape=jax.ShapeDtypeStruct((b, s, d), jnp.float32),
        scratch_shapes=[
            pltpu.VMEM((tm + 2 * POOL_HALO, width), jnp.float32),
            pltpu.VMEM((tm, width), jnp.float32),
        ],
        compiler_params=_params("parallel", "parallel"),
        name="pool_mix_output",
    )(ug, ug, ug, ug, x, gate, grp_w, ch_scale, w_out, ln_g, ln_b)


def _rope_tables(s, total_rows, q_scale):
    n_grid_rows = s // GRID_W
    inv_freq = ROPE_THETA ** (-jnp.arange(ROPE_PAIRS, dtype=jnp.float32) / ROPE_PAIRS)
    ang_r = jnp.arange(n_grid_rows, dtype=jnp.float32)[:, None] * inv_freq
    ang_c = jnp.arange(GRID_W, dtype=jnp.float32)[:, None] * inv_freq
    cr, sr = (jnp.repeat(f(ang_r), GRID_W, axis=0) for f in (jnp.cos, jnp.sin))
    cc, sc = (jnp.tile(f(ang_c), (n_grid_rows, 1)) for f in (jnp.cos, jnp.sin))
    c = jnp.concatenate([cr, cc, cr, cc], axis=-1)
    sn = jnp.concatenate([-sr, -sc, sr, sc], axis=-1)
    rest = total_rows - s
    c = jnp.concatenate([c, jnp.ones((rest, LANES), jnp.float32)], axis=0)
    sn = jnp.concatenate([sn, jnp.zeros((rest, LANES), jnp.float32)], axis=0)
    return tuple(jnp.stack([tab * q_scale, tab]) for tab in (c, sn))


def _pair_rotation_partners(w):
    d, n = w.shape
    w = w.reshape(d, n // HEAD_DIM, 2, 2, ROPE_PAIRS)
    return jnp.swapaxes(w, 2, 3).reshape(d, n)


def kernel(x, c, ctx, c_ctx, mod_w, mod_b, ln_g, ln_b, attn_w_in, attn_w_out, attn_lq1, attn_lk1, attn_lq2,
           attn_lk2, attn_subln_g, pool_w_in, pool_grp_w, pool_scale, pool_w_out):
    b, s, d = x.shape
    ctx_len = ctx.shape[1]
    depth = mod_w.shape[0]
    assert d == N_HEADS * V_DIM and b + 1 <= MOD_ROWS
    assert s % PROJ_ROWS == 0 and ctx_len <= PROJ_ROWS
    alpha = (2 * depth) ** 0.25
    q_scale = LOG2E / math.sqrt(HEAD_DIM)
    bf16 = jnp.bfloat16

    cvec = jnp.concatenate([c, c_ctx[None, :], jnp.zeros((MOD_ROWS - b - 1, d), jnp.float32)], axis=0)
    mod = _modulation(cvec, mod_w, mod_b)
    tabs = _rope_tables(s, s + PROJ_ROWS, q_scale)

    for i in range(depth):
        is_attn = i % 2 == 0
        ctx_out = any(j % 2 == 0 for j in range(i + 1, depth))
        shift, scale, gate = (mod[i, :, k * d:(k + 1) * d] for k in range(3))
        sh_x, sc_x, gt_x = (v[:b, None, :] for v in (shift, scale, gate))
        sh_c, sc_c, gt_c = (jnp.broadcast_to(v[b][None, None, :], (b, 1, d)) for v in (shift, scale, gate))
        lg, lb = ln_g[i][None, :], ln_b[i][None, :]
        if is_attn:
            a = i // 2
            lam_init = 0.8 - 0.6 * math.exp(-0.3 * i)
            w_in = attn_w_in[a].astype(bf16)
            w_qk, w_vt, w_gate = _pair_rotation_partners(w_in[:, :2 * d]), w_in[:, 2 * d:3 * d].T, w_in[:, 3 * d:]
            w_out = attn_w_out[a].astype(bf16)
            lams = [v[a][None, :] for v in (attn_lq1, attn_lk1, attn_lq2, attn_lk2)]
            subg = attn_subln_g[a][None, :]
            total = s + ctx_len
            sh_2, sc_2 = (jnp.stack([vx, vc], axis=1) for vx, vc in ((sh_x, sh_c), (sc_x, sc_c)))
            ctx_pad = jnp.pad(ctx, ((0, 0), (0, PROJ_ROWS - ctx_len), (0, 0)))
            qk = _project_qk(x, ctx_pad, sh_2, sc_2, w_qk, tabs)
            vt = _project_vt(x, ctx_pad, sh_2, sc_2, w_vt)
            g_x = _project_f32(x, sh_x, sc_x, w_gate)
            y_x = _attention(qk, vt, g_x, lams, subg, lam_init, (0, s), (0, total))
            if ctx_out:
                g_c = _project_f32(ctx, sh_c, sc_c, w_gate)
                y_c = _attention(qk, vt, g_c, lams, subg, lam_init, (s, ctx_len), (s, ctx_len))
                ctx = _output_layer_norm(y_c, ctx, gt_c, w_out, lg, lb, alpha)
            x = _output_layer_norm(y_x, x, gt_x, w_out, lg, lb, alpha)
        else:
            p = i // 2
            w_in = pool_w_in[p].astype(bf16)
            grp_w = pool_grp_w[p].astype(bf16)
            w_out = pool_w_out[p].astype(bf16)
            cs = pool_scale[p][None, :]
            ug_x = _project_f32(x, sh_x, sc_x, w_in)
            if ctx_out:
                ug_c = _project_f32(ctx, sh_c, sc_c, w_in)
                ctx = _pool_mix_output(ug_c, ctx, gt_c, grp_w, cs, w_out, lg, lb, alpha)
            x = _pool_mix_output(ug_x, x, gt_x, grp_w, cs, w_out, lg, lb, alpha)
    return x
```

```python
import functools
import math

import jax
import jax.numpy as jnp
from jax import lax
from jax.experimental import pallas as pl
from jax.experimental.pallas import tpu as pltpu

N_HEADS = 8
HEAD_DIM = 128
V_DIM = 2 * HEAD_DIM
GRID_W = 64
ROPE_THETA = 10000.0
ROPE_PAIRS = HEAD_DIM // 4
POOL_WINDOWS = (2, 4, 8, 16)
POOL_HALO = 8
LN_EPS = 1e-5
SUBLN_EPS = 1e-5

LANES = 128
SUBLANES = 8
MXU_DEPTH = 256
VMEM_LIMIT_BYTES = 60 * 1024 * 1024

PROJ_ROWS = 512
PROJ_COLS = 2048
ATTN_Q_ROWS = 2048
ATTN_K_ROWS = 1280
ATTN_KEY_STEP = 3328
ATTN_DEFERRED_PIECES = 2
MOD_COLS = 1024
MOD_ROWS = SUBLANES

LOG2E = 1.4426950408889634


def _silu(v):
    return v / (1.0 + jnp.exp(-v))


def _params(*semantics):
    return pltpu.CompilerParams(dimension_semantics=semantics, vmem_limit_bytes=VMEM_LIMIT_BYTES)


def _mod_kernel(c_ref, w_ref, b_ref, o_ref):
    s = _silu(c_ref[...])
    o_ref[...] = jnp.dot(s, w_ref[...], precision=lax.Precision.HIGHEST,
                         preferred_element_type=jnp.float32) + b_ref[...]


def _modulation(cvec, mod_w, mod_b):
    depth, d, n = mod_w.shape
    tn = min(MOD_COLS, n)
    return pl.pallas_call(
        _mod_kernel,
        grid=(depth, n // tn),
        in_specs=[
            pl.BlockSpec((MOD_ROWS, d), lambda l, j: (0, 0)),
            pl.BlockSpec((None, d, tn), lambda l, j: (l, 0, j)),
            pl.BlockSpec((None, 1, tn), lambda l, j: (l, 0, j)),
        ],
        out_specs=pl.BlockSpec((None, MOD_ROWS, tn), lambda l, j: (l, 0, j)),
        out_shape=jax.ShapeDtypeStruct((depth, MOD_ROWS, n), jnp.float32),
        compiler_params=_params("parallel", "parallel"),
        name="modulation",
    )(cvec, mod_w, mod_b.reshape(depth, 1, n))


def _modulated_matmul(x_ref, sh_ref, sc_ref, w_ref):
    h = (x_ref[...] * (1.0 + sc_ref[...]) + sh_ref[...]).astype(jnp.bfloat16)
    return jnp.dot(h, w_ref[...], preferred_element_type=jnp.float32)


def _proj_f32_kernel(x_ref, sh_ref, sc_ref, w_ref, o_ref):
    o_ref[...] = _modulated_matmul(x_ref, sh_ref, sc_ref, w_ref)


def _modulated_tokens(x_ref, ctx_ref, sh_ref, sc_ref, step, n_latent_steps):
    tokens = jnp.where(step == n_latent_steps, ctx_ref[...], x_ref[...])
    return (tokens * (1.0 + sc_ref[...]) + sh_ref[...]).astype(jnp.bfloat16)


def _token_specs(tm, d, n_latent_steps, grid_rank):
    def lift(fn):
        return (lambda n, bb, i: fn(bb, i)) if grid_rank == 3 else fn
    return [
        pl.BlockSpec((None, tm, d), lift(lambda bb, i: (bb, jnp.minimum(i, n_latent_steps - 1), 0))),
        pl.BlockSpec((None, tm, d), lift(lambda bb, i: (bb, 0, 0))),
        pl.BlockSpec((None, None, 1, d), lift(lambda bb, i: (bb, i // n_latent_steps, 0, 0))),
        pl.BlockSpec((None, None, 1, d), lift(lambda bb, i: (bb, i // n_latent_steps, 0, 0))),
    ]


def _proj_qk_kernel(x_ref, ctx_ref, sh_ref, sc_ref, w_ref, c_ref, s_ref, o_ref):
    h = _modulated_tokens(x_ref, ctx_ref, sh_ref, sc_ref, pl.program_id(2), pl.num_programs(2) - 1)
    c, s = c_ref[...], s_ref[...]
    half = w_ref.shape[1] // 2
    for part in range(2):
        acc = jnp.dot(h, w_ref[:, part * half:(part + 1) * half], preferred_element_type=jnp.float32)
        for gl in range(half // LANES):
            t = acc[:, gl * LANES:(gl + 1) * LANES]
            r = t * c + pltpu.roll(t, LANES // 2, 1) * s
            head, comp = divmod(part * (half // LANES) + gl, V_DIM // LANES)
            o_ref[head, :, comp * LANES:(comp + 1) * LANES] = r.astype(o_ref.dtype)


def _project_qk(x, ctx, shift, scale, w_qk, tables):
    b, s, d = x.shape
    tn = PROJ_COLS
    nt = w_qk.shape[1] // tn
    nh = tn // V_DIM
    tm = PROJ_ROWS
    nx = s // tm
    tab_spec = pl.BlockSpec((None, tm, LANES), lambda n, bb, i: (n, i, 0))
    return pl.pallas_call(
        _proj_qk_kernel,
        grid=(nt, b, nx + 1),
        in_specs=_token_specs(tm, d, nx, 3) + [
            pl.BlockSpec((d, tn), lambda n, bb, i: (0, n)),
            tab_spec, tab_spec,
        ],
        out_specs=pl.BlockSpec((None, None, nh, tm, V_DIM), lambda n, bb, i: (n, bb, 0, i, 0)),
        out_shape=jax.ShapeDtypeStruct((nt, b, nh, (nx + 1) * tm, V_DIM), jnp.bfloat16),
        compiler_params=_params("arbitrary", "arbitrary", "arbitrary"),
        name="project_qk",
    )(x, ctx, shift, scale, w_qk, *tables)


def _proj_vt_kernel(x_ref, ctx_ref, sh_ref, sc_ref, wt_ref, o_ref):
    h = _modulated_tokens(x_ref, ctx_ref, sh_ref, sc_ref, pl.program_id(1), pl.num_programs(1) - 1)
    vt = lax.dot_general(wt_ref[...], h, (((1,), (1,)), ((), ())),
                         preferred_element_type=jnp.float32)
    for head in range(o_ref.shape[0]):
        o_ref[head] = vt[head * V_DIM:(head + 1) * V_DIM, :].astype(o_ref.dtype)


def _project_vt(x, ctx, shift, scale, w_vt):
    b, s, d = x.shape
    nh = w_vt.shape[0] // V_DIM
    tm = PROJ_ROWS
    nx = s // tm
    return pl.pallas_call(
        _proj_vt_kernel,
        grid=(b, nx + 1),
        in_specs=_token_specs(tm, d, nx, 2) + [pl.BlockSpec(w_vt.shape, lambda bb, i: (0, 0))],
        out_specs=pl.BlockSpec((None, nh, V_DIM, tm), lambda bb, i: (bb, 0, 0, i)),
        out_shape=jax.ShapeDtypeStruct((b, nh, V_DIM, (nx + 1) * tm), jnp.bfloat16),
        compiler_params=_params("parallel", "arbitrary"),
        name="project_vt",
    )(x, ctx, shift, scale, w_vt)


def _project_f32(x, shift, scale, w):
    b, s, d = x.shape
    tn = PROJ_COLS
    nt = w.shape[1] // tn
    tm = min(PROJ_ROWS, s)
    return pl.pallas_call(
        _proj_f32_kernel,
        grid=(nt, b, s // tm),
        in_specs=[
            pl.BlockSpec((None, tm, d), lambda n, bb, i: (bb, i, 0)),
            pl.BlockSpec((None, 1, d), lambda n, bb, i: (bb, 0, 0)),
            pl.BlockSpec((None, 1, d), lambda n, bb, i: (bb, 0, 0)),
            pl.BlockSpec((d, tn), lambda n, bb, i: (0, n)),
        ],
        out_specs=pl.BlockSpec((None, tm, tn), lambda n, bb, i: (bb, i, n)),
        out_shape=jax.ShapeDtypeStruct((b, s, nt * tn), jnp.float32),
        compiler_params=_params("arbitrary", "arbitrary", "arbitrary"),
        name="project_f32",
    )(x, shift, scale, w)


def _key_plan(skv):
    step = MXU_DEPTH
    for cand in range(MXU_DEPTH, min(ATTN_KEY_STEP, skv) + 1, MXU_DEPTH):
        if skv % cand == 0:
            step = cand
    subs = [ATTN_K_ROWS] * (step // ATTN_K_ROWS)
    if step % ATTN_K_ROWS:
        subs.append(step % ATTN_K_ROWS)
    return step, tuple(subs)


def _attn_kernel(par_ref, qc_ref, qn_ref, kc_ref, kn_ref, vt_ref, g_ref, o_ref,
                 s_sc, mx_sc, m_sc, l_sc, acc_sc, *, lam_init, subs):
    i = pl.program_id(2)
    j = pl.program_id(3)
    tq = qc_ref.shape[0]
    n_query_chunks = tq // MXU_DEPTH
    starts = [sum(subs[:u]) for u in range(len(subs))]

    def scores_piece(k_src_ref, row0, n_rows, q_src_ref, comp, qc):
        cols = slice(comp * HEAD_DIM, (comp + 1) * HEAD_DIM)
        qs = slice(qc * MXU_DEPTH, (qc + 1) * MXU_DEPTH)
        st = lax.dot_general(k_src_ref[row0:row0 + n_rows, cols], q_src_ref[qs, cols], (((1,), (1,)), ((), ())),
                             preferred_element_type=jnp.float32)
        s_sc[comp, 0:n_rows, qs] = st
        mx_sc[comp, :, qs] = jnp.max(st.reshape(n_rows // SUBLANES, SUBLANES, MXU_DEPTH), axis=0)

    def consume_piece(row0, n_rows, comp, qc):
        qs = slice(qc * MXU_DEPTH, (qc + 1) * MXU_DEPTH)
        m_prev = m_sc[comp, :, qs]
        m_new = jnp.maximum(m_prev, jnp.max(mx_sc[comp, :, qs], axis=0, keepdims=True))
        a = jnp.exp2(m_prev - m_new)
        m_sc[comp, :, qs] = m_new
        st = s_sc[comp, 0:n_rows, qs].reshape(n_rows // SUBLANES, SUBLANES, MXU_DEPTH)
        p = jnp.exp2(st - jnp.broadcast_to(m_new, (SUBLANES, MXU_DEPTH)))
        l_sc[comp, :, qs] = a * l_sc[comp, :, qs] + jnp.sum(p, axis=0)
        pv = jnp.dot(vt_ref[:, row0:row0 + n_rows], p.reshape(n_rows, MXU_DEPTH).astype(jnp.bfloat16),
                     preferred_element_type=jnp.float32)
        acc_sc[comp, :, qs] = a * acc_sc[comp, :, qs] + pv

    order = [(comp, qc) for qc in range(n_query_chunks) for comp in range(2)]
    n_deferred = min(ATTN_DEFERRED_PIECES, len(order) - 1)
    n_ahead = len(order) - n_deferred

    def sub_step(u):
        if u + 1 < len(subs):
            look_ahead = (kc_ref, starts[u + 1], subs[u + 1], qc_ref)
            n_look = len(order)
        else:
            look_ahead = (kn_ref, 0, subs[0], qn_ref)
            n_look = n_ahead
        for n, (comp, qc) in enumerate(order):
            if u == 0 and n < n_deferred:
                scores_piece(kc_ref, 0, subs[0], qc_ref, *order[n_ahead + n])
            consume_piece(starts[u], subs[u], comp, qc)
            if 0 < n <= n_look:
                scores_piece(*look_ahead, *order[n - 1])
        if n_look == len(order):
            scores_piece(*look_ahead, *order[-1])

    @pl.when((i == 0) & (j == 0))
    def _():
        for comp, qc in order[:n_ahead]:
            scores_piece(kc_ref, 0, subs[0], qc_ref, comp, qc)

    @pl.when(j == 0)
    def _():
        m_sc[...] = jnp.full(m_sc.shape, -jnp.inf, jnp.float32)
        l_sc[...] = jnp.zeros(l_sc.shape, jnp.float32)
        acc_sc[...] = jnp.zeros(acc_sc.shape, jnp.float32)

    for u in range(len(subs)):
        sub_step(u)

    @pl.when(j == pl.num_programs(3) - 1)
    def _():
        lq1, lk1, lq2, lk2 = (par_ref[r:r + 1, 0:HEAD_DIM] for r in range(4))
        lam = (jnp.exp(jnp.sum(lq1 * lk1, axis=-1, keepdims=True))
               - jnp.exp(jnp.sum(lq2 * lk2, axis=-1, keepdims=True)) + lam_init)
        l1 = jnp.sum(l_sc[0], axis=0, keepdims=True)
        l2 = jnp.sum(l_sc[1], axis=0, keepdims=True)
        ot = acc_sc[0] / l1 - lam * (acc_sc[1] / l2)
        ot = ot * lax.rsqrt(jnp.mean(ot * ot, axis=0, keepdims=True) + SUBLN_EPS)
        o = ot.T * par_ref[4:5, :] * (1.0 - lam_init)
        o_ref[...] = (o * _silu(g_ref[...])).astype(o_ref.dtype)


def _attention(qk, vt, gate_path, lams, subln_g, lam_init, q_rows, kv_rows):
    _, b, nh, _, _ = qk.shape
    (q_start, sq), (kv_start, skv) = q_rows, kv_rows
    tq = min(ATTN_Q_ROWS, sq)
    tk, subs = _key_plan(skv)
    assert sq % tq == 0 and q_start % tq == 0 and kv_start % tk == 0
    nq, nk = sq // tq, skv // tk
    q0, k0 = q_start // tq, kv_start // tk
    pad = jnp.zeros((1, V_DIM - HEAD_DIM), jnp.float32)
    params = jnp.concatenate([jnp.concatenate([v, pad], axis=1) for v in lams] + [subln_g]
                             + [jnp.zeros((SUBLANES - 5, V_DIM), jnp.float32)], axis=0)
    return pl.pallas_call(
        functools.partial(_attn_kernel, lam_init=lam_init, subs=subs),
        grid=(b, nh, nq, nk),
        in_specs=[
            pl.BlockSpec((SUBLANES, V_DIM), lambda bb, h, i, j: (0, 0)),
            pl.BlockSpec((None, None, None, tq, V_DIM), lambda bb, h, i, j: (0, bb, h, q0 + i, 0)),
            pl.BlockSpec((None, None, None, tq, V_DIM),
                         lambda bb, h, i, j: (0, bb, h, q0 + jnp.minimum(i + (j + 1) // nk, nq - 1), 0)),
            pl.BlockSpec((None, None, None, tk, V_DIM), lambda bb, h, i, j: (1, bb, h, k0 + j, 0)),
            pl.BlockSpec((None, None, None, tk, V_DIM), lambda bb, h, i, j: (1, bb, h, k0 + (j + 1) % nk, 0)),
            pl.BlockSpec((None, None, V_DIM, tk), lambda bb, h, i, j: (bb, h, 0, k0 + j)),
            pl.BlockSpec((None, tq, V_DIM), lambda bb, h, i, j: (bb, i, h)),
        ],
        out_specs=pl.BlockSpec((None, tq, V_DIM), lambda bb, h, i, j: (bb, i, h)),
        out_shape=jax.ShapeDtypeStruct((b, sq, nh * V_DIM), jnp.bfloat16),
        scratch_shapes=[
            pltpu.VMEM((2, max(subs), tq), jnp.float32),
            pltpu.VMEM((2, SUBLANES, tq), jnp.float32),
            pltpu.VMEM((2, 1, tq), jnp.float32),
            pltpu.VMEM((2, SUBLANES, tq), jnp.float32),
            pltpu.VMEM((2, V_DIM, tq), jnp.float32),
        ],
        compiler_params=_params("parallel", "parallel", "arbitrary", "arbitrary"),
        name="diff_attention",
    )(params, qk, qk, qk, qk, vt, gate_path)


def _residual_layer_norm(x, gate, y, ln_g, ln_b, alpha):
    z = alpha * x + gate * y
    mu = jnp.mean(z, axis=-1, keepdims=True)
    zc = z - mu
    var = jnp.mean(zc * zc, axis=-1, keepdims=True)
    return zc * lax.rsqrt(var + LN_EPS) * ln_g + ln_b


def _out_kernel(y_ref, x_ref, gate_ref, w_ref, lg_ref, lb_ref, o_ref, *, alpha):
    half = y_ref.shape[0] // 2
    for rows in (slice(0, half), slice(half, 2 * half)):
        y = jnp.dot(y_ref[rows, :], w_ref[...], preferred_element_type=jnp.float32)
        o_ref[rows, :] = _residual_layer_norm(x_ref[rows, :], gate_ref[...], y, lg_ref[...], lb_ref[...], alpha)


def _output_layer_norm(y, x, gate, w_out, ln_g, ln_b, alpha):
    b, s, d = x.shape
    tm = min(PROJ_ROWS, s)
    vec = pl.BlockSpec((1, d), lambda bb, i: (0, 0))
    return pl.pallas_call(
        functools.partial(_out_kernel, alpha=alpha),
        grid=(b, s // tm),
        in_specs=[
            pl.BlockSpec((None, tm, y.shape[2]), lambda bb, i: (bb, i, 0)),
            pl.BlockSpec((None, tm, d), lambda bb, i: (bb, i, 0)),
            pl.BlockSpec((None, 1, d), lambda bb, i: (bb, 0, 0)),
            pl.BlockSpec(w_out.shape, lambda bb, i: (0, 0)),
            vec, vec,
        ],
        out_specs=pl.BlockSpec((None, tm, d), lambda bb, i: (bb, i, 0)),
        out_shape=jax.ShapeDtypeStruct((b, s, d), jnp.float32),
        compiler_params=_params("parallel", "parallel"),
        name="output_layer_norm",
    )(y, x, gate, w_out, ln_g, ln_b)


def _pool_kernel(u_ref, up_ref, un_ref, g_ref, x_ref, gate_ref, gw_ref, cs_ref, w_ref, lg_ref, lb_ref,
                 o_ref, ubuf, mix, *, seq, alpha):
    i = pl.program_id(1)
    tm = u_ref.shape[0]
    gdim = u_ref.shape[1] // len(POOL_WINDOWS)
    ubuf[0:POOL_HALO, :] = jnp.where(i > 0, up_ref[...], 0.0)
    ubuf[POOL_HALO:POOL_HALO + tm, :] = u_ref[...]
    ubuf[POOL_HALO + tm:2 * POOL_HALO + tm, :] = jnp.where(i < pl.num_programs(1) - 1, un_ref[...], 0.0)
    t = i * tm + lax.broadcasted_iota(jnp.int32, (tm, 1), 0)
    rows = tm + 2 * POOL_HALO
    for gi, w in enumerate(POOL_WINDOWS):
        lo = w // 2
        hi = w - 1 - lo
        assert hi == lo - 1 and lo & (lo - 1) == 0 and lo <= POOL_HALO
        cols = slice(gi * gdim, (gi + 1) * gdim)
        fwd = ubuf[:, cols]
        span = 1
        while span < lo:
            fwd = fwd + pltpu.roll(fwd, rows - span, 0)
            span *= 2
        win = (fwd + pltpu.roll(fwd, lo, 0))[POOL_HALO:POOL_HALO + tm]
        cnt = (jnp.minimum(t + hi + 1, seq) - jnp.maximum(t - lo, 0)).astype(jnp.float32)
        pooled = win / cnt - u_ref[:, cols]
        mix[:, cols] = jnp.dot(pooled.astype(jnp.bfloat16), gw_ref[gi], preferred_element_type=jnp.float32)
    for rows in (slice(0, tm // 2), slice(tm // 2, tm)):
        gated = ((mix[rows, :] * cs_ref[...]) * _silu(g_ref[rows, :])).astype(jnp.bfloat16)
        y = jnp.dot(gated, w_ref[...], preferred_element_type=jnp.float32)
        o_ref[rows, :] = _residual_layer_norm(x_ref[rows, :], gate_ref[...], y, lg_ref[...], lb_ref[...], alpha)


def _pool_mix_output(ug, x, gate, grp_w, ch_scale, w_out, ln_g, ln_b, alpha):
    b, s, d = x.shape
    width = ug.shape[2] // 2
    tm = min(PROJ_ROWS, s)
    halo_blocks = tm // POOL_HALO
    last_halo = s // POOL_HALO - 1
    vec = pl.BlockSpec((1, d), lambda bb, i: (0, 0))
    return pl.pallas_call(
        functools.partial(_pool_kernel, seq=s, alpha=alpha),
        grid=(b, s // tm),
        in_specs=[
            pl.BlockSpec((None, tm, width), lambda bb, i: (bb, i, 0)),
            pl.BlockSpec((None, POOL_HALO, width), lambda bb, i: (bb, jnp.maximum(i * halo_blocks - 1, 0), 0)),
            pl.BlockSpec((None, POOL_HALO, width),
                         lambda bb, i: (bb, jnp.minimum((i + 1) * halo_blocks, last_halo), 0)),
            pl.BlockSpec((None, tm, width), lambda bb, i: (bb, i, 1)),
            pl.BlockSpec((None, tm, d), lambda bb, i: (bb, i, 0)),
            pl.BlockSpec((None, 1, d), lambda bb, i: (bb, 0, 0)),
            pl.BlockSpec(grp_w.shape, lambda bb, i: (0, 0, 0)),
            pl.BlockSpec((1, width), lambda bb, i: (0, 0)),
            pl.BlockSpec(w_out.shape, lambda bb, i: (0, 0)),
            vec, vec,
        ],
        out_specs=pl.BlockSpec((None, tm, d), lambda bb, i: (bb, i, 0)),
        out_shape=jax.ShapeDtypeStruct((b, s, d), jnp.float32),
        scratch_shapes=[
            pltpu.VMEM((tm + 2 * POOL_HALO, width), jnp.float32),
            pltpu.VMEM((tm, width), jnp.float32),
        ],
        compiler_params=_params("parallel", "parallel"),
        name="pool_mix_output",
    )(ug, ug, ug, ug, x, gate, grp_w, ch_scale, w_out, ln_g, ln_b)


def _rope_tables(s, total_rows, q_scale):
    n_grid_rows = s // GRID_W
    inv_freq = ROPE_THETA ** (-jnp.arange(ROPE_PAIRS, dtype=jnp.float32) / ROPE_PAIRS)
    ang_r = jnp.arange(n_grid_rows, dtype=jnp.float32)[:, None] * inv_freq
    ang_c = jnp.arange(GRID_W, dtype=jnp.float32)[:, None] * inv_freq
    cr, sr = (jnp.repeat(f(ang_r), GRID_W, axis=0) for f in (jnp.cos, jnp.sin))
    cc, sc = (jnp.tile(f(ang_c), (n_grid_rows, 1)) for f in (jnp.cos, jnp.sin))
    c = jnp.concatenate([cr, cc, cr, cc], axis=-1)
    sn = jnp.concatenate([-sr, -sc, sr, sc], axis=-1)
    rest = total_rows - s
    c = jnp.concatenate([c, jnp.ones((rest, LANES), jnp.float32)], axis=0)
    sn = jnp.concatenate([sn, jnp.zeros((rest, LANES), jnp.float32)], axis=0)
    return tuple(jnp.stack([tab * q_scale, tab]) for tab in (c, sn))


def _pair_rotation_partners(w):
    d, n = w.shape
    w = w.reshape(d, n // HEAD_DIM, 2, 2, ROPE_PAIRS)
    return jnp.swapaxes(w, 2, 3).reshape(d, n)


def kernel(x, c, ctx, c_ctx, mod_w, mod_b, ln_g, ln_b, attn_w_in, attn_w_out, attn_lq1, attn_lk1, attn_lq2,
           attn_lk2, attn_subln_g, pool_w_in, pool_grp_w, pool_scale, pool_w_out):
    b, s, d = x.shape
    ctx_len = ctx.shape[1]
    depth = mod_w.shape[0]
    assert d == N_HEADS * V_DIM and b + 1 <= MOD_ROWS
    assert s % PROJ_ROWS == 0 and ctx_len <= PROJ_ROWS
    alpha = (2 * depth) ** 0.25
    q_scale = LOG2E / math.sqrt(HEAD_DIM)
    bf16 = jnp.bfloat16

    cvec = jnp.concatenate([c, c_ctx[None, :], jnp.zeros((MOD_ROWS - b - 1, d), jnp.float32)], axis=0)
    mod = _modulation(cvec, mod_w, mod_b)
    tabs = _rope_tables(s, s + PROJ_ROWS, q_scale)

    for i in range(depth):
        is_attn = i % 2 == 0
        ctx_out = any(j % 2 == 0 for j in range(i + 1, depth))
        shift, scale, gate = (mod[i, :, k * d:(k + 1) * d] for k in range(3))
        sh_x, sc_x, gt_x = (v[:b, None, :] for v in (shift, scale, gate))
        sh_c, sc_c, gt_c = (jnp.broadcast_to(v[b][None, None, :], (b, 1, d)) for v in (shift, scale, gate))
        lg, lb = ln_g[i][None, :], ln_b[i][None, :]
        if is_attn:
            a = i // 2
            lam_init = 0.8 - 0.6 * math.exp(-0.3 * i)
            w_in = attn_w_in[a].astype(bf16)
            w_qk, w_vt, w_gate = _pair_rotation_partners(w_in[:, :2 * d]), w_in[:, 2 * d:3 * d].T, w_in[:, 3 * d:]
            w_out = attn_w_out[a].astype(bf16)
            lams = [v[a][None, :] for v in (attn_lq1, attn_lk1, attn_lq2, attn_lk2)]
            subg = attn_subln_g[a][None, :]
            total = s + ctx_len
            sh_2, sc_2 = (jnp.stack([vx, vc], axis=1) for vx, vc in ((sh_x, sh_c), (sc_x, sc_c)))
            ctx_pad = jnp.pad(ctx, ((0, 0), (0, PROJ_ROWS - ctx_len), (0, 0)))
            qk = _project_qk(x, ctx_pad, sh_2, sc_2, w_qk, tabs)
            vt = _project_vt(x, ctx_pad, sh_2, sc_2, w_vt)
            g_x = _project_f32(x, sh_x, sc_x, w_gate)
            y_x = _attention(qk, vt, g_x, lams, subg, lam_init, (0, s), (0, total))
            if ctx_out:
                g_c = _project_f32(ctx, sh_c, sc_c, w_gate)
                y_c = _attention(qk, vt, g_c, lams, subg, lam_init, (s, ctx_len), (s, ctx_len))
                ctx = _output_layer_norm(y_c, ctx, gt_c, w_out, lg, lb, alpha)
            x = _output_layer_norm(y_x, x, gt_x, w_out, lg, lb, alpha)
        else:
            p = i // 2
            w_in = pool_w_in[p].astype(bf16)
            grp_w = pool_grp_w[p].astype(bf16)
            w_out = pool_w_out[p].astype(bf16)
            cs = pool_scale[p][None, :]
            ug_x = _project_f32(x, sh_x, sc_x, w_in)
            if ctx_out:
                ug_c = _project_f32(ctx, sh_c, sc_c, w_in)
                ctx = _pool_mix_output(ug_c, ctx, gt_c, grp_w, cs, w_out, lg, lb, alpha)
            x = _pool_mix_output(ug_x, x, gt_x, grp_w, cs, w_out, lg, lb, alpha)
    return x
```
